```python
import numpy as np
import jax
import jax.numpy as jnp
from jax import lax

D_MODEL = 1024
BATCH = 2
SEQ = 8192
DEPTH = 4
DEC_BATCH = 32
DEC_SEQ = 4
PAST_LEN = 8192
PAGE_SIZE = 128

N_HEADS_A = 16
HEAD_DIM_A = 64
N_KV_A = 4
GROUP_A = N_HEADS_A // N_KV_A
D_A = N_HEADS_A * HEAD_DIM_A
CMP_STRIDE = 16
CMP_BLOCK = 2 * CMP_STRIDE
CMP_HIDDEN = 128
SEL_BLOCK = 64
N_SEL = 16
WINDOW = 512
Q_BLOCK = 128
FORCE_SCORE = 1000.0
D_B = D_MODEL
CONV_WIDTH = 31
D_C = D_MODEL
POOL_WINDOWS = (2, 4, 8, 16)
N_POOL_GROUPS = len(POOL_WINDOWS)
POOL_GROUP = D_C // N_POOL_GROUPS
POOL_HIST = max(POOL_WINDOWS) - 1
MEM_LEN = 256
N_HEADS_M = 4
HEAD_DIM_M = D_MODEL // N_HEADS_M
N_A_LAYERS = (DEPTH + 1) // 2
N_C_LAYERS = DEPTH // 2
KV_BRANCH = 2 * N_KV_A * HEAD_DIM_A
EVEN_SPLITS = (D_A, 3 * KV_BRANCH, 3 * N_HEADS_A, D_A, 2 * D_B, D_B)
EVEN_IN = sum(EVEN_SPLITS)
NORM_EPS = 1e-6

kernel_name = "nsa_conformer_pool_hybrid_step"


def rms_norm(x, g):
    x32 = x.astype(jnp.float32)
    y = x32 * lax.rsqrt(jnp.mean(x32 * x32, axis=-1, keepdims=True) + NORM_EPS)
    return (y * g.astype(jnp.float32)).astype(x.dtype)


def layer_norm(x, g, b):
    x32 = x.astype(jnp.float32)
    xc = x32 - jnp.mean(x32, axis=-1, keepdims=True)
    y = xc * lax.rsqrt(jnp.mean(xc * xc, axis=-1, keepdims=True) + NORM_EPS)
    return (y * g.astype(jnp.float32) + b.astype(jnp.float32)).astype(x.dtype)


def masked_softmax(s, mask, axis):
    s = jnp.where(mask, s.astype(jnp.float32), -jnp.inf)
    m = jnp.max(s, axis=axis, keepdims=True)
    m = jnp.where(jnp.isfinite(m), m, 0.0)
    e = jnp.exp(s - m)
    d = jnp.sum(e, axis=axis, keepdims=True)
    return e / jnp.where(d > 0, d, 1.0)


def compress_kv(kv, pe, w1, w2):
    B, T = kv.shape[:2]
    halves = kv.reshape(B, T // CMP_STRIDE, CMP_STRIDE, 2, N_KV_A, HEAD_DIM_A)
    bias = jnp.einsum('cld,cldh->ch', pe, w1)
    first = jnp.einsum('bnlckd,cldh->bnckh', halves, w1[:, :CMP_STRIDE])
    second = jnp.einsum('bnlckd,cldh->bnckh', halves, w1[:, CMP_STRIDE:])
    hid = jax.nn.silu(first[:, :-1] + second[:, 1:] + bias[:, None, :])
    return jnp.einsum('bnckh,chd->bnckd', hid, w2)


def cmp_to_sel_overlap(n_cmp, n_sel):
    c0 = jnp.arange(n_cmp) * CMP_STRIDE
    s0 = jnp.arange(n_sel) * SEL_BLOCK
    ov = (jnp.minimum(c0[:, None] + CMP_BLOCK, s0[None, :] + SEL_BLOCK)
          - jnp.maximum(c0[:, None], s0[None, :]))
    return jnp.clip(ov, 0, None).astype(jnp.float32) / CMP_BLOCK


def nsa_core(q, pos_q, gates, ckv, skv, wkv, pos_w):
    dt = q.dtype
    q = q * (HEAD_DIM_A ** -0.5)
    n_cmp, n_sel = ckv.shape[1], skv.shape[1]
    cmp_end = jnp.arange(n_cmp) * CMP_STRIDE + (CMP_BLOCK - 1)
    cmask = (cmp_end[None, :] <= pos_q[:, None])[None, :, None, None, :]
    s = jnp.einsum('bqkgd,bnkd->bqkgn', q, ckv[:, :, 0])
    p_cmp = masked_softmax(s, cmask, -1)
    o_cmp = jnp.einsum('bqkgn,bnkd->bqkgd', p_cmp.astype(dt), ckv[:, :, 1])
    p_slc = jnp.einsum('bqkgn,nj->bqkj', p_cmp, cmp_to_sel_overlap(n_cmp, n_sel))
    blk = jnp.arange(n_sel)
    cur = pos_q // SEL_BLOCK
    valid = blk[None, :] * SEL_BLOCK <= pos_q[:, None]
    forced = (blk[None, :] == 0) | (blk[None, :] == cur[:, None]) | (blk[None, :] == cur[:, None] - 1)
    score = jnp.where(valid[None, :, None], jnp.where(forced[None, :, None], FORCE_SCORE, p_slc), -jnp.inf)
    top_s, idx = lax.top_k(score, min(N_SEL, n_sel))
    bi = jnp.arange(q.shape[0])[:, None, None, None]
    hi = jnp.arange(N_KV_A)[None, None, :, None]
    rows = skv[bi, idx, :, :, hi]
    pos_k = idx[..., None] * SEL_BLOCK + jnp.arange(SEL_BLOCK)
    smask = (pos_k <= pos_q[None, :, None, None, None]) & jnp.isfinite(top_s)[..., None]
    s = jnp.einsum('bqkgd,bqknsd->bqkgns', q, rows[..., 0, :])
    p = masked_softmax(s, smask[:, :, :, None], (-2, -1))
    o_sel = jnp.einsum('bqkgns,bqknsd->bqkgd', p.astype(dt), rows[..., 1, :])
    wmask = ((pos_w[None, :] <= pos_q[:, None]) & (pos_q[:, None] - pos_w[None, :] < WINDOW)
             & (pos_w[None, :] >= 0))
    s = jnp.einsum('bqkgd,blkd->bqkgl', q, wkv[:, :, 0])
    p = masked_softmax(s, wmask[None, :, None, None, :], -1)
    o_win = jnp.einsum('bqkgl,blkd->bqkgd', p.astype(dt), wkv[:, :, 1])
    return (gates[:, :, 0, :, :, None] * o_cmp + gates[:, :, 1, :, :, None] * o_sel
            + gates[:, :, 2, :, :, None] * o_win)


def nsa_prompt(q, gates, kv, pe, w1, w2):
    B, T = q.shape[:2]
    ckv = compress_kv(kv[:, :, 0], pe, w1, w2)
    skv = kv[:, :, 1].reshape(B, T // SEL_BLOCK, SEL_BLOCK, 2, N_KV_A, HEAD_DIM_A)
    wkv = jnp.pad(kv[:, :, 2], ((0, 0), (WINDOW, 0), (0, 0), (0, 0), (0, 0)))

    def block(i):
        start = i * Q_BLOCK
        pos_q = start + jnp.arange(Q_BLOCK)
        pos_w = start - WINDOW + jnp.arange(Q_BLOCK + WINDOW)
        return nsa_core(lax.dynamic_slice_in_dim(q, start, Q_BLOCK, 1), pos_q,
                        lax.dynamic_slice_in_dim(gates, start, Q_BLOCK, 1), ckv, skv,
                        lax.dynamic_slice_in_dim(wkv, start, Q_BLOCK + WINDOW, 1), pos_w)

    out = lax.map(block, jnp.arange(T // Q_BLOCK))
    return jnp.moveaxis(out, 0, 1).reshape(B, T, D_A)


def nsa_sample(q, gates, kv, cache_cmp, cache_sel, win_buf, page_table, layer, pe, w1, w2):
    DB, T = q.shape[:2]
    past = page_table.shape[1] * PAGE_SIZE
    pad = (-(past + T)) % SEL_BLOCK

    def history(cache, new):
        rows = cache[layer, page_table].reshape(DB, past, 2, N_KV_A, HEAD_DIM_A)
        return jnp.concatenate([rows, new, jnp.zeros((DB, pad) + new.shape[2:], new.dtype)], axis=1)

    ckv = compress_kv(history(cache_cmp, kv[:, :, 0]), pe, w1, w2)
    skv = history(cache_sel, kv[:, :, 1]).reshape(DB, -1, SEL_BLOCK, 2, N_KV_A, HEAD_DIM_A)
    wb = win_buf.shape[1]
    wkv = jnp.concatenate([win_buf, kv[:, :, 2]], axis=1)
    pos_q = past + jnp.arange(T)
    pos_w = past - wb + jnp.arange(wb + T)
    out = nsa_core(q, pos_q, gates, ckv, skv, wkv, pos_w).reshape(DB, T, D_A)
    return out, wkv[:, T:]


def conformer_conv(u_hist, u, w, b, ln_g, ln_b):
    full = jnp.concatenate([u_hist, u], axis=1)
    y = lax.conv_general_dilated(full, w[:, None, :], (1,), 'VALID',
                                 dimension_numbers=('NWC', 'WIO', 'NWC'),
                                 feature_group_count=u.shape[-1]) + b
    return jax.nn.silu(layer_norm(y, ln_g, ln_b)), full[:, -(CONV_WIDTH - 1):]


def split_even(z):
    B, T = z.shape[:2]
    cuts = np.cumsum(EVEN_SPLITS)[:-1].tolist()
    q, kv, g, gate_a, glu, gate_b = jnp.split(z, cuts, axis=-1)
    q = q.reshape(B, T, N_KV_A, GROUP_A, HEAD_DIM_A)
    kv = kv.reshape(B, T, 3, 2, N_KV_A, HEAD_DIM_A)
    g = jax.nn.sigmoid(g).reshape(B, T, 3, N_KV_A, GROUP_A)
    val, gl = jnp.split(glu, 2, axis=-1)
    return q, kv, g, gate_a, val * jax.nn.sigmoid(gl), gate_b


def merge_even(o_a, gate_a, o_b, gate_b, w_out):
    return jnp.concatenate([o_a * jax.nn.silu(gate_a), o_b * jax.nn.silu(gate_b)], axis=-1) @ w_out


def multiscale_pool(u_hist, u, p0):
    full = jnp.concatenate([u_hist, u], axis=1)
    B, L, C = full.shape
    T = u.shape[1]
    cs = jnp.concatenate([jnp.zeros((B, 1, C), jnp.float32),
                          jnp.cumsum(full.astype(jnp.float32), axis=1)], axis=1)
    i = jnp.arange(L - T, L)
    hi = i + 1
    means = []
    for g, w in enumerate(POOL_WINDOWS):
        lo = jnp.maximum(hi - w, 0)
        ch = slice(g * POOL_GROUP, (g + 1) * POOL_GROUP)
        cnt = jnp.minimum(w, p0 + i + 1).astype(jnp.float32)
        means.append((cs[:, hi, ch] - cs[:, lo, ch]) / cnt[None, :, None])
    d = jnp.concatenate(means, axis=-1) - u.astype(jnp.float32)
    return d.astype(u.dtype), full[:, -POOL_HIST:]


def pool_mixer(h, u_hist, p0, w_in, pool_w, pool_scale, w_out):
    u, gate = jnp.split(h @ w_in, 2, axis=-1)
    d, new_hist = multiscale_pool(u_hist, u, p0)
    B, T = d.shape[:2]
    m = jnp.einsum('btgc,gcd->btgd', d.reshape(B, T, N_POOL_GROUPS, POOL_GROUP), pool_w).reshape(B, T, D_C)
    return (m * pool_scale * jax.nn.silu(gate)) @ w_out, new_hist


def memory_kv(mem, w):
    B, M = mem.shape[:2]
    return (mem @ w).reshape(B, M, 2, N_HEADS_M, HEAD_DIM_M)


def memory_attend(h, mkv, wq, wo):
    B, T = h.shape[:2]
    q = (h @ wq).reshape(B, T, N_HEADS_M, HEAD_DIM_M) * (HEAD_DIM_M ** -0.5)
    s = jnp.einsum('bthd,bmhd->bhtm', q, mkv[:, :, 0]).astype(jnp.float32)
    p = jax.nn.softmax(s, axis=-1).astype(h.dtype)
    o = jnp.einsum('bhtm,bmhd->bthd', p, mkv[:, :, 1]).reshape(B, T, N_HEADS_M * HEAD_DIM_M)
    return o @ wo


def setup_inputs(seed: int = 0) -> dict:
    key = jax.random.key(seed)
    ks = jax.random.split(key, 32)
    counter = [0]

    def nxt():
        counter[0] += 1
        return ks[counter[0] - 1]

    def normal(shape, scale=1.0):
        return scale * jax.random.normal(nxt(), shape, jnp.float32)

    def gain(shape):
        return 1.0 + 0.02 * jax.random.normal(nxt(), shape, jnp.float32)

    n_pages = PAST_LEN // PAGE_SIZE
    n_used = DEC_BATCH * n_pages
    n_pool = n_used + max(1, n_used // 4)
    win_buf = min(WINDOW, PAST_LEN)
    kv_row = (2, N_KV_A, HEAD_DIM_A)
    page_table = jax.random.permutation(nxt(), n_pool)[:n_used].reshape(DEC_BATCH, n_pages).astype(jnp.int32)
    return {
        'x_prompt': normal((BATCH, SEQ, D_MODEL)),
        'x_sample': normal((DEC_BATCH, DEC_SEQ, D_MODEL)),
        'mem_prompt': normal((BATCH, MEM_LEN, D_MODEL)),
        'cache_cmp_kv': normal((N_A_LAYERS, n_pool, PAGE_SIZE) + kv_row),
        'cache_sel_kv': normal((N_A_LAYERS, n_pool, PAGE_SIZE) + kv_row),
        'cache_win_kv': normal((N_A_LAYERS, DEC_BATCH, win_buf) + kv_row),
        'state_conv': normal((N_A_LAYERS, DEC_BATCH, CONV_WIDTH - 1, D_B)),
        'state_pool': normal((N_C_LAYERS, DEC_BATCH, POOL_HIST, D_C)),
        'cache_mem_kv': normal((DEPTH, DEC_BATCH, MEM_LEN, 2, N_HEADS_M, HEAD_DIM_M)),
        'page_table': page_table,
        'norm_mix': gain((DEPTH, D_MODEL)),
        'norm_mem': gain((DEPTH, D_MODEL)),
        'norm_out': gain((D_MODEL,)),
        'w_in_a': normal((N_A_LAYERS, D_MODEL, EVEN_IN), D_MODEL ** -0.5),
        'w_out_a': normal((N_A_LAYERS, D_A + D_B, D_MODEL), (D_A + D_B) ** -0.5),
        'cmp_pe': normal((N_A_LAYERS, 2, CMP_BLOCK, HEAD_DIM_A), 0.1),
        'cmp_w1': normal((N_A_LAYERS, 2, CMP_BLOCK, HEAD_DIM_A, CMP_HIDDEN), (CMP_BLOCK * HEAD_DIM_A) ** -0.5),
        'cmp_w2': normal((N_A_LAYERS, 2, CMP_HIDDEN, HEAD_DIM_A), CMP_HIDDEN ** -0.5),
        'conv_w': normal((N_A_LAYERS, CONV_WIDTH, D_B), CONV_WIDTH ** -0.5),
        'conv_b': normal((N_A_LAYERS, D_B), 0.02),
        'conv_ln_g': gain((N_A_LAYERS, D_B)),
        'conv_ln_b': normal((N_A_LAYERS, D_B), 0.02),
        'w_in_c': normal((N_C_LAYERS, D_MODEL, 2 * D_C), D_MODEL ** -0.5),
        'pool_w': normal((N_C_LAYERS, N_POOL_GROUPS, POOL_GROUP, POOL_GROUP), POOL_GROUP ** -0.5),
        'pool_scale': gain((N_C_LAYERS, D_C)),
        'w_out_c': normal((N_C_LAYERS, D_C, D_MODEL), D_C ** -0.5),
        'w_mem_q': normal((DEPTH, D_MODEL, D_MODEL), D_MODEL ** -0.5),
        'w_mem_kv': normal((DEPTH, D_MODEL, 2 * D_MODEL), D_MODEL ** -0.5),
        'w_mem_o': normal((DEPTH, D_MODEL, D_MODEL), D_MODEL ** -0.5),
    }


def reference(x_prompt, x_sample, mem_prompt, cache_cmp_kv, cache_sel_kv, cache_win_kv,
              state_conv, state_pool, cache_mem_kv, page_table, norm_mix, norm_mem, norm_out,
              w_in_a, w_out_a, cmp_pe, cmp_w1, cmp_w2, conv_w, conv_b, conv_ln_g, conv_ln_b,
              w_in_c, pool_w, pool_scale, w_out_c, w_mem_q, w_mem_kv, w_mem_o):
    xp, xs = x_prompt, x_sample
    B, T = xp.shape[:2]
    past = page_table.shape[1] * PAGE_SIZE
    p_cmp, p_sel, p_win, p_conv, p_pool, p_mem = [], [], [], [], [], []
    s_cmp, s_sel, s_win, s_conv, s_pool = [], [], [], [], []
    for layer in range(DEPTH):
        hp = rms_norm(xp, norm_mix[layer])
        hs = rms_norm(xs, norm_mix[layer])
        if layer % 2 == 0:
            a = layer // 2
            conv_args = (conv_w[a], conv_b[a], conv_ln_g[a], conv_ln_b[a])
            q, kv, g, ga, u, gb = split_even(hp @ w_in_a[a])
            o_a = nsa_prompt(q, g, kv, cmp_pe[a], cmp_w1[a], cmp_w2[a])
            o_b, conv_new = conformer_conv(jnp.zeros((B, CONV_WIDTH - 1, D_B), u.dtype), u, *conv_args)
            xp = xp + merge_even(o_a, ga, o_b, gb, w_out_a[a])
            p_cmp.append(kv[:, :, 0])
            p_sel.append(kv[:, :, 1])
            p_win.append(kv[:, T - min(WINDOW, T):, 2])
            p_conv.append(conv_new)
            q, kv, g, ga, u, gb = split_even(hs @ w_in_a[a])
            o_a, win_new = nsa_sample(q, g, kv, cache_cmp_kv, cache_sel_kv, cache_win_kv[a], page_table, a,
                                      cmp_pe[a], cmp_w1[a], cmp_w2[a])
            o_b, conv_new = conformer_conv(state_conv[a], u, *conv_args)
            xs = xs + merge_even(o_a, ga, o_b, gb, w_out_a[a])
            s_cmp.append(kv[:, :, 0])
            s_sel.append(kv[:, :, 1])
            s_win.append(win_new)
            s_conv.append(conv_new)
        else:
            c = layer // 2
            pool_args = (w_in_c[c], pool_w[c], pool_scale[c], w_out_c[c])
            o, pool_new = pool_mixer(hp, jnp.zeros((B, 0, D_C), hp.dtype), 0, *pool_args)
            xp = xp + o
            p_pool.append(pool_new)
            o, pool_new = pool_mixer(hs, state_pool[c], past - state_pool.shape[2], *pool_args)
            xs = xs + o
            s_pool.append(pool_new)
        mkv = memory_kv(mem_prompt, w_mem_kv[layer])
        p_mem.append(mkv)
        xp = xp + memory_attend(rms_norm(xp, norm_mem[layer]), mkv, w_mem_q[layer], w_mem_o[layer])
        xs = xs + memory_attend(rms_norm(xs, norm_mem[layer]), cache_mem_kv[layer], w_mem_q[layer], w_mem_o[layer])
    y_prompt = rms_norm(xp, norm_out)
    y_sample = rms_norm(xs, norm_out)
    return (y_prompt, y_sample,
            jnp.stack(p_cmp), jnp.stack(p_sel), jnp.stack(p_win), jnp.stack(p_conv), jnp.stack(p_pool),
            jnp.stack(p_mem),
            jnp.stack(s_cmp), jnp.stack(s_sel), jnp.stack(s_win), jnp.stack(s_conv), jnp.stack(s_pool))
```

```python
import functools

import jax
import jax.numpy as jnp
from jax import lax
from jax.experimental import pallas as pl
from jax.experimental.pallas import tpu as pltpu

F32 = jnp.float32
BF16 = jnp.bfloat16

LANES = 128
SUBLANES = 8
VMEM_LIMIT_BYTES = 56 * 1024 * 1024

D_MODEL = 1024
N_HEADS_A = 16
HEAD_DIM_A = 64
N_KV_A = 4
GROUP_A = N_HEADS_A // N_KV_A
D_A = N_HEADS_A * HEAD_DIM_A
KV_W = N_KV_A * HEAD_DIM_A
KV_BRANCH = 2 * KV_W
CMP_STRIDE = 16
CMP_BLOCK = 32
CMP_HIDDEN = 128
SEL_BLOCK = 64
N_SEL = 16
WINDOW = 512
Q_BLOCK = 128
FORCE_SCORE = 1000.0
CONV_WIDTH = 31
CONV_HALO = 32
POOL_WINDOWS = (2, 4, 8, 16)
POOL_GROUP = D_MODEL // len(POOL_WINDOWS)
POOL_HIST = 15
POOL_HALO = 16
PAGE_SIZE = 128
N_HEADS_M = 4
HEAD_DIM_M = 256
NORM_EPS = 1e-6
NEG = -1e30

COL_Q, COL_GA, COL_VAL, COL_GL, COL_GB = 0, 1024, 2048, 3072, 4096
COL_KV = 5120
COL_G = COL_KV + 3 * KV_BRANCH
EVEN_N = 6912
EVEN_TN = 768
PAGES_PER_STEP = 8


def _cparams(sem):
    return pltpu.CompilerParams(dimension_semantics=sem, vmem_limit_bytes=VMEM_LIMIT_BYTES)


def _silu(x):
    return x * jax.nn.sigmoid(x)


def _mm_kernel(*refs, n_parts, has_gain, has_res):
    parts = refs[:n_parts]
    pos = n_parts
    gain_ref = refs[pos] if has_gain else None
    pos += int(has_gain)
    w_ref = refs[pos]
    pos += 1
    res_ref = refs[pos] if has_res else None
    pos += int(has_res)
    o_ref, lhs = refs[pos], refs[pos + 1]

    @pl.when(pl.program_id(1) == 0)
    def _():
        off = 0
        for p in parts:
            x = p[...].astype(F32)
            if has_gain:
                x = x * lax.rsqrt(jnp.mean(x * x, axis=-1, keepdims=True) + NORM_EPS) * gain_ref[...]
            lhs[:, off:off + p.shape[1]] = x.astype(BF16)
            off += p.shape[1]

    acc = jnp.dot(lhs[...], w_ref[...], preferred_element_type=F32)
    if has_res:
        acc = acc + res_ref[...]
    o_ref[...] = acc.astype(o_ref.dtype)


def _mm(parts, w, *, gain=None, res=None, tm, tn, out_dtype=F32):
    M = parts[0].shape[0]
    K, N = w.shape
    assert sum(p.shape[1] for p in parts) == K and M % tm == 0 and N % tn == 0
    in_specs = [pl.BlockSpec((tm, p.shape[1]), lambda i, j: (i, 0)) for p in parts]
    args = list(parts)
    if gain is not None:
        in_specs.append(pl.BlockSpec((1, K), lambda i, j: (0, 0)))
        args.append(gain.reshape(1, K).astype(F32))
    in_specs.append(pl.BlockSpec((K, tn), lambda i, j: (0, j)))
    args.append(w)
    if res is not None:
        in_specs.append(pl.BlockSpec((tm, tn), lambda i, j: (i, j)))
        args.append(res)
    return pl.pallas_call(
        functools.partial(_mm_kernel, n_parts=len(parts), has_gain=gain is not None, has_res=res is not None),
        out_shape=jax.ShapeDtypeStruct((M, N), out_dtype),
        grid=(M // tm, N // tn),
        in_specs=in_specs,
        out_specs=pl.BlockSpec((tm, tn), lambda i, j: (i, j)),
        scratch_shapes=[pltpu.VMEM((tm, K), BF16)],
        compiler_params=_cparams(("parallel", "arbitrary")),
    )(*args)


def _rmsnorm_kernel(x_ref, g_ref, o_ref):
    x = x_ref[...]
    o_ref[...] = x * lax.rsqrt(jnp.mean(x * x, axis=-1, keepdims=True) + NORM_EPS) * g_ref[...]


def _rmsnorm(x, g, *, tm):
    M, D = x.shape
    return pl.pallas_call(
        _rmsnorm_kernel,
        out_shape=jax.ShapeDtypeStruct((M, D), F32),
        grid=(M // tm,),
        in_specs=[pl.BlockSpec((tm, D), lambda i: (i, 0)), pl.BlockSpec((1, D), lambda i: (0, 0))],
        out_specs=pl.BlockSpec((tm, D), lambda i: (i, 0)),
        compiler_params=_cparams(("parallel",)),
    )(x, g.reshape(1, D))


def _glu_kernel(val_ref, gl_ref, o_ref):
    o_ref[...] = val_ref[...] * jax.nn.sigmoid(gl_ref[...])


def _glu(z, *, tt):
    B, T, _ = z.shape
    blk = (None, tt, D_MODEL)
    return pl.pallas_call(
        _glu_kernel,
        out_shape=jax.ShapeDtypeStruct((B, T, D_MODEL), F32),
        grid=(B, T // tt),
        in_specs=[pl.BlockSpec(blk, lambda b, i: (b, i, COL_VAL // D_MODEL)),
                  pl.BlockSpec(blk, lambda b, i: (b, i, COL_GL // D_MODEL))],
        out_specs=pl.BlockSpec(blk, lambda b, i: (b, i, 0)),
        compiler_params=_cparams(("parallel", "parallel")),
    )(z, z)


CONV_ROWS = 32


def _conv_kernel(halo_ref, cur_ref, gate_ref, w_ref, b_ref, lg_ref, lb_ref, o_ref, buf, y_scr, *, tt, zero_first_halo):
    halo = halo_ref[...]
    if zero_first_halo:
        halo = jnp.where(pl.program_id(1) == 0, 0.0, halo)
    buf[0:CONV_HALO, :] = halo
    buf[CONV_HALO:CONV_HALO + tt, :] = cur_ref[...]
    first = CONV_HALO - (CONV_WIDTH - 1)
    rc = min(CONV_ROWS, tt)
    for r0 in range(0, tt, rc):
        acc = jnp.broadcast_to(b_ref[...], (rc, D_MODEL))
        for j in range(CONV_WIDTH):
            acc = acc + w_ref[j:j + 1, :] * buf[first + r0 + j:first + r0 + j + rc, :]
        y_scr[r0:r0 + rc, :] = acc
    y = y_scr[...]
    xc = y - jnp.mean(y, axis=-1, keepdims=True)
    yn = xc * lax.rsqrt(jnp.mean(xc * xc, axis=-1, keepdims=True) + NORM_EPS) * lg_ref[...] + lb_ref[...]
    o_ref[...] = (_silu(yn) * _silu(gate_ref[...])).astype(o_ref.dtype)


def _conv(halo_arr, halo_map, zero_first_halo, u, gate_arr, gate_col, w, b, lg, lb, *, tt):
    B, T, _ = u.shape
    w32 = jnp.concatenate([w, jnp.zeros((CONV_HALO - CONV_WIDTH + 1, D_MODEL), F32)], axis=0)[:CONV_HALO]
    vec = pl.BlockSpec((1, D_MODEL), lambda bb, i: (0, 0))
    return pl.pallas_call(
        functools.partial(_conv_kernel, tt=tt, zero_first_halo=zero_first_halo),
        out_shape=jax.ShapeDtypeStruct((B, T, D_MODEL), BF16),
        grid=(B, T // tt),
        in_specs=[pl.BlockSpec((None, CONV_HALO, D_MODEL), halo_map),
                  pl.BlockSpec((None, tt, D_MODEL), lambda bb, i: (bb, i, 0)),
                  pl.BlockSpec((None, tt, D_MODEL), lambda bb, i: (bb, i, gate_col)),
                  pl.BlockSpec((CONV_HALO, D_MODEL), lambda bb, i: (0, 0)),
                  vec, vec, vec],
        out_specs=pl.BlockSpec((None, tt, D_MODEL), lambda bb, i: (bb, i, 0)),
        scratch_shapes=[pltpu.VMEM((CONV_HALO + tt, D_MODEL), F32), pltpu.VMEM((tt, D_MODEL), F32)],
        compiler_params=_cparams(("parallel", "parallel")),
    )(halo_arr, u, gate_arr, w32, b.reshape(1, -1), lg.reshape(1, -1), lb.reshape(1, -1))


def _pool_kernel(halo_ref, cur_ref, gate_ref, pw_ref, ps_ref, o_ref, buf, *, tt, zero_first_halo, pos0):
    i = pl.program_id(1)
    halo = halo_ref[...]
    if zero_first_halo:
        halo = jnp.where(i == 0, 0.0, halo)
    buf[0:POOL_HALO, :] = halo
    buf[POOL_HALO:POOL_HALO + tt, :] = cur_ref[...]
    pos = pos0 + i * tt + lax.broadcasted_iota(jnp.int32, (tt, 1), 0)
    for g, w in enumerate(POOL_WINDOWS):
        c0, c1 = g * POOL_GROUP, (g + 1) * POOL_GROUP
        s = buf[POOL_HALO:POOL_HALO + tt, c0:c1]
        for j in range(1, w):
            s = s + buf[POOL_HALO - j:POOL_HALO - j + tt, c0:c1]
        cnt = jnp.minimum(w, pos + 1).astype(F32)
        d = s / cnt - buf[POOL_HALO:POOL_HALO + tt, c0:c1]
        m = jnp.dot(d.astype(BF16), pw_ref[g], preferred_element_type=F32)
        o_ref[:, c0:c1] = (m * ps_ref[:, c0:c1] * _silu(gate_ref[:, c0:c1])).astype(o_ref.dtype)


def _pool(halo_arr, halo_map, zero_first_halo, z, pool_w, pool_scale, *, tt, pos0):
    B, T, _ = z.shape
    return pl.pallas_call(
        functools.partial(_pool_kernel, tt=tt, zero_first_halo=zero_first_halo, pos0=pos0),
        out_shape=jax.ShapeDtypeStruct((B, T, D_MODEL), BF16),
        grid=(B, T // tt),
        in_specs=[pl.BlockSpec((None, POOL_HALO, D_MODEL), halo_map),
                  pl.BlockSpec((None, tt, D_MODEL), lambda bb, i: (bb, i, 0)),
                  pl.BlockSpec((None, tt, D_MODEL), lambda bb, i: (bb, i, 1)),
                  pl.BlockSpec((len(POOL_WINDOWS), POOL_GROUP, POOL_GROUP), lambda bb, i: (0, 0, 0)),
                  pl.BlockSpec((1, D_MODEL), lambda bb, i: (0, 0))],
        out_specs=pl.BlockSpec((None, tt, D_MODEL), lambda bb, i: (bb, i, 0)),
        scratch_shapes=[pltpu.VMEM((POOL_HALO + tt, D_MODEL), F32)],
        compiler_params=_cparams(("parallel", "parallel")),
    )(halo_arr, z, z, pool_w.astype(BF16), pool_scale.reshape(1, -1))


def _mem_attn_kernel(q_ref, kt_ref, v_ref, o_ref):
    for h in range(N_HEADS_M):
        c0, c1 = h * HEAD_DIM_M, (h + 1) * HEAD_DIM_M
        q = q_ref[:, c0:c1] * (HEAD_DIM_M ** -0.5)
        s = jnp.dot(q, kt_ref[c0:c1, :], preferred_element_type=F32)
        e = jnp.exp(s - jnp.max(s, axis=-1, keepdims=True))
        p = e / jnp.sum(e, axis=-1, keepdims=True)
        o = jnp.dot(p.astype(BF16), v_ref[:, c0:c1], preferred_element_type=F32)
        o_ref[:, c0:c1] = o.astype(o_ref.dtype)


def _mem_attn(q, kt, v, *, tm):
    NB, Tq, D = q.shape
    M = v.shape[1]
    return pl.pallas_call(
        _mem_attn_kernel,
        out_shape=jax.ShapeDtypeStruct((NB, Tq, D), BF16),
        grid=(NB, Tq // tm),
        in_specs=[pl.BlockSpec((None, tm, D), lambda b, i: (b, i, 0)),
                  pl.BlockSpec((None, D, M), lambda b, i: (b, 0, 0)),
                  pl.BlockSpec((None, M, D), lambda b, i: (b, 0, 0))],
        out_specs=pl.BlockSpec((None, tm, D), lambda b, i: (b, i, 0)),
        compiler_params=_cparams(("parallel", "parallel")),
    )(q, kt, v)


def _compress_weights(pe, w1, w2):
    eye2 = jnp.eye(2, dtype=F32)

    def bd(wc):
        return jnp.einsum('ldh,kK->lkdKh', wc, eye2).reshape(CMP_STRIDE * LANES, 2 * CMP_HIDDEN)

    wfs = jnp.stack([jnp.concatenate([bd(w1[c, :CMP_STRIDE]), bd(w1[c, CMP_STRIDE:])], axis=1)
                     for c in range(2) for _ in range(2)]).astype(BF16)
    w2bd = jnp.stack([jnp.einsum('hd,kK->khKd', w2[c], eye2).reshape(2 * CMP_HIDDEN, LANES)
                      for c in range(2)]).astype(BF16)
    pe_rows = jnp.zeros((2, 2 * SUBLANES, CMP_BLOCK * HEAD_DIM_A), F32).at[:, 0].set(pe.reshape(2, -1))
    bias = jnp.stack([_mm([pe_rows[c]], w1[c].reshape(CMP_BLOCK * HEAD_DIM_A, CMP_HIDDEN).astype(BF16),
                          tm=2 * SUBLANES, tn=CMP_HIDDEN)[0] for c in range(2)])
    bias2 = jnp.concatenate([bias, bias], axis=1).reshape(2, 1, 2 * CMP_HIDDEN)
    return wfs, w2bd, bias2


def _compress_kernel(tab_ref, *refs, n_rows, n_steps):
    pages = refs[:PAGES_PER_STEP]
    wfs_ref, bias_ref, w2_ref, o_ref, xs, sec = refs[PAGES_PER_STEP:]
    s = pl.program_id(1)
    for r in range(PAGES_PER_STEP):
        row = pl.multiple_of((s * PAGES_PER_STEP + r) * SUBLANES, SUBLANES)
        xs[pl.ds(row, SUBLANES), :] = pages[r][...]

    @pl.when(s == 0)
    def _():
        xs[n_rows:n_rows + SUBLANES, :] = jnp.zeros((SUBLANES, xs.shape[1]), F32)

    @pl.when(s == n_steps - 1)
    def _():
        row_w = CMP_STRIDE * KV_BRANCH
        for grp in range(4):
            c, kp = divmod(grp, 2)
            base = c * KV_W + kp * LANES
            xg = jnp.concatenate([xs[:, l * KV_BRANCH + base:l * KV_BRANCH + base + LANES]
                                  for l in range(CMP_STRIDE)], axis=1).astype(BF16)
            assert xg.shape[1] * 4 == row_w
            fs = jnp.dot(xg, wfs_ref[grp], preferred_element_type=F32)
            sec[...] = fs[:, 2 * CMP_HIDDEN:]
            hid = _silu(fs[0:n_rows, :2 * CMP_HIDDEN] + sec[1:n_rows + 1, :] + bias_ref[c])
            o_ref[:, base:base + LANES] = jnp.dot(hid.astype(BF16), w2_ref[c], preferred_element_type=F32)


def _compress(cache_view, table, wfs, w2bd, bias2):
    NB, n_pages = table.shape
    n_steps = n_pages // PAGES_PER_STEP
    n_rows = n_pages * SUBLANES
    row_w = cache_view.shape[2]

    def page_spec(r):
        return pl.BlockSpec((None, SUBLANES, row_w), lambda b, s, tab: (tab[b, s * PAGES_PER_STEP + r], 0, 0))

    const3 = lambda b, s, tab: (0, 0, 0)
    return pl.pallas_call(
        functools.partial(_compress_kernel, n_rows=n_rows, n_steps=n_steps),
        out_shape=jax.ShapeDtypeStruct((NB, n_rows, KV_BRANCH), F32),
        grid_spec=pltpu.PrefetchScalarGridSpec(
            num_scalar_prefetch=1,
            grid=(NB, n_steps),
            in_specs=[page_spec(r) for r in range(PAGES_PER_STEP)] + [
                pl.BlockSpec(wfs.shape, const3), pl.BlockSpec(bias2.shape, const3), pl.BlockSpec(w2bd.shape, const3)],
            out_specs=pl.BlockSpec((None, n_rows, KV_BRANCH), lambda b, s, tab: (b, 0, 0)),
            scratch_shapes=[pltpu.VMEM((n_rows + SUBLANES, row_w), F32),
                            pltpu.VMEM((n_rows + SUBLANES, 2 * CMP_HIDDEN), F32)]),
        compiler_params=_cparams(("parallel", "arbitrary")),
    )(table, *([cache_view] * PAGES_PER_STEP), wfs, bias2, w2bd)


def _overlap(n_cmp_rows, n_sel_cols):
    c0 = jnp.arange(n_cmp_rows) * CMP_STRIDE
    s0 = jnp.arange(n_sel_cols) * SEL_BLOCK
    ov = jnp.minimum(c0[:, None] + CMP_BLOCK, s0[None, :] + SEL_BLOCK) - jnp.maximum(c0[:, None], s0[None, :])
    return (jnp.clip(ov, 0, None).astype(F32) / CMP_BLOCK).astype(BF16)


def _block_expand(n_blocks, n_keys):
    return (jnp.arange(n_keys)[None, :] // SEL_BLOCK == jnp.arange(n_blocks)[:, None]).astype(BF16)


def _masked_softmax(s, mask, axis=-1):
    m = jnp.max(jnp.where(mask, s, NEG), axis=axis, keepdims=True)
    e = jnp.where(mask, jnp.exp(s - m), 0.0)
    d = jnp.sum(e, axis=axis, keepdims=True)
    return e / jnp.where(d > 0, d, 1.0)


def _top_blocks(score, idx, axis):
    n = score.shape[axis]
    sel = jnp.zeros(score.shape, F32)
    work = score
    for _ in range(N_SEL):
        m = jnp.max(work, axis=axis, keepdims=True)
        first = jnp.min(jnp.where(work == m, idx, n), axis=axis, keepdims=True)
        pick = idx == first
        sel = jnp.where(pick & (m > -jnp.inf), 1.0, sel)
        work = jnp.where(pick, -jnp.inf, work)
    return sel


SEL_TK = 256
WIN_KEYS = WINDOW + Q_BLOCK


def _nsa_prompt_kernel(q_ref, ga_ref, g_ref, ckt_ref, cv_ref, skt_ref, sv_ref, wkt_ref, wv_ref, e_ref, ov_ref,
                       o_ref, m_scr, l_scr, acc_scr):
    i = pl.program_id(1)
    q0 = i * Q_BLOCK
    n_cmp = ckt_ref.shape[1]
    pos_q = q0 + lax.broadcasted_iota(jnp.int32, (Q_BLOCK, 1), 0)
    gates = jax.nn.sigmoid(g_ref[...])
    j_t = lax.broadcasted_iota(jnp.int32, (LANES, Q_BLOCK), 0)
    pos_t = q0 + lax.broadcasted_iota(jnp.int32, (LANES, Q_BLOCK), 1)
    cur_t = pos_t // SEL_BLOCK
    valid_t = j_t * SEL_BLOCK <= pos_t
    forced_t = (j_t == 0) | (j_t == cur_t) | (j_t == cur_t - 1)
    cmp_end = lax.broadcasted_iota(jnp.int32, (1, n_cmp), 1) * CMP_STRIDE + (CMP_BLOCK - 1)
    cmask = cmp_end <= pos_q
    w0 = pl.multiple_of(jnp.maximum(q0 - WINDOW, 0), Q_BLOCK)
    pos_w = w0 + lax.broadcasted_iota(jnp.int32, (1, WIN_KEYS), 1)
    wmask = (pos_w <= pos_q) & (pos_q - pos_w < WINDOW)
    n_tiles = (q0 + Q_BLOCK + SEL_TK - 1) // SEL_TK

    for k in range(N_KV_A):
        kr = slice(k * HEAD_DIM_A, (k + 1) * HEAD_DIM_A)
        qk = jnp.concatenate([q_ref[:, (k * GROUP_A + g) * HEAD_DIM_A:(k * GROUP_A + g + 1) * HEAD_DIM_A]
                              for g in range(GROUP_A)], axis=0)
        qk = (qk * (HEAD_DIM_A ** -0.5)).astype(BF16)

        s = jnp.dot(qk, ckt_ref[kr, :], preferred_element_type=F32).reshape(GROUP_A, Q_BLOCK, n_cmp)
        p = _masked_softmax(s, cmask[None]).astype(BF16).reshape(GROUP_A * Q_BLOCK, n_cmp)
        o_cmp = jnp.dot(p, cv_ref[:, kr], preferred_element_type=F32)
        pj = jnp.dot(p, ov_ref[...], preferred_element_type=F32)
        p_slc = pj[0:Q_BLOCK] + pj[Q_BLOCK:2 * Q_BLOCK] + pj[2 * Q_BLOCK:3 * Q_BLOCK] + pj[3 * Q_BLOCK:]

        score_t = jnp.where(valid_t, jnp.where(forced_t, FORCE_SCORE, p_slc.T), -jnp.inf)
        sel = _top_blocks(score_t, j_t, 0).T.astype(BF16)

        m_scr[...] = jnp.full(m_scr.shape, NEG, F32)
        l_scr[...] = jnp.zeros(l_scr.shape, F32)
        acc_scr[...] = jnp.zeros(acc_scr.shape, F32)

        def sel_step(t, carry, qk=qk, sel=sel, kr=kr):
            k0 = pl.multiple_of(t * SEL_TK, SEL_TK)
            s = jnp.dot(qk, skt_ref[kr, pl.ds(k0, SEL_TK)], preferred_element_type=F32)
            s = s.reshape(GROUP_A, Q_BLOCK, SEL_TK)
            picked = jnp.dot(sel, e_ref[:, pl.ds(k0, SEL_TK)], preferred_element_type=F32)
            pos_k = k0 + lax.broadcasted_iota(jnp.int32, (1, SEL_TK), 1)
            mask = ((picked > 0.5) & (pos_k <= pos_q))[None]
            m_old = m_scr[...]
            m_new = jnp.maximum(m_old, jnp.max(jnp.where(mask, s, NEG), axis=-1, keepdims=True))
            alpha = jnp.exp(m_old - m_new)
            pe = jnp.where(mask, jnp.exp(s - m_new), 0.0)
            l_scr[...] = alpha * l_scr[...] + jnp.sum(pe, axis=-1, keepdims=True)
            pv = jnp.dot(pe.astype(BF16).reshape(GROUP_A * Q_BLOCK, SEL_TK), sv_ref[pl.ds(k0, SEL_TK), kr],
                         preferred_element_type=F32)
            acc_scr[...] = alpha * acc_scr[...] + pv.reshape(GROUP_A, Q_BLOCK, HEAD_DIM_A)
            m_scr[...] = m_new
            return carry

        lax.fori_loop(0, n_tiles, sel_step, 0)
        l_fin = l_scr[...]
        o_sel = acc_scr[...] / jnp.where(l_fin > 0, l_fin, 1.0)

        s = jnp.dot(qk, wkt_ref[kr, pl.ds(w0, WIN_KEYS)], preferred_element_type=F32)
        p = _masked_softmax(s.reshape(GROUP_A, Q_BLOCK, WIN_KEYS), wmask[None]).astype(BF16)
        o_win = jnp.dot(p.reshape(GROUP_A * Q_BLOCK, WIN_KEYS), wv_ref[pl.ds(w0, WIN_KEYS), kr],
                        preferred_element_type=F32)

        for g in range(GROUP_A):
            h = k * GROUP_A + g
            rows = slice(g * Q_BLOCK, (g + 1) * Q_BLOCK)
            hc = slice(h * HEAD_DIM_A, (h + 1) * HEAD_DIM_A)
            o = (gates[:, h:h + 1] * o_cmp[rows] + gates[:, N_HEADS_A + h:N_HEADS_A + h + 1] * o_sel[g]
                 + gates[:, 2 * N_HEADS_A + h:2 * N_HEADS_A + h + 1] * o_win[rows])
            o_ref[:, hc] = (o * _silu(ga_ref[:, hc])).astype(o_ref.dtype)


def _nsa_prompt(z, ckt, cv, skt, sv, wkt, wv):
    B, T, _ = z.shape
    n_cmp = ckt.shape[2]
    e = _block_expand(LANES, T)
    ov = _overlap(n_cmp, LANES)
    once = pl.Buffered(1)

    def whole(shape):
        return pl.BlockSpec((None,) + shape, lambda b, i: (b, 0, 0), pipeline_mode=once)

    return pl.pallas_call(
        _nsa_prompt_kernel,
        out_shape=jax.ShapeDtypeStruct((B, T, D_A), BF16),
        grid=(B, T // Q_BLOCK),
        in_specs=[pl.BlockSpec((None, Q_BLOCK, D_A), lambda b, i: (b, i, COL_Q // D_A)),
                  pl.BlockSpec((None, Q_BLOCK, D_A), lambda b, i: (b, i, COL_GA // D_A)),
                  pl.BlockSpec((None, Q_BLOCK, LANES), lambda b, i: (b, i, COL_G // LANES)),
                  whole((KV_W, n_cmp)), whole((n_cmp, KV_W)),
                  whole((KV_W, T)), whole((T, KV_W)), whole((KV_W, T)), whole((T, KV_W)),
                  pl.BlockSpec((LANES, T), lambda b, i: (0, 0), pipeline_mode=once),
                  pl.BlockSpec((n_cmp, LANES), lambda b, i: (0, 0), pipeline_mode=once)],
        out_specs=pl.BlockSpec((None, Q_BLOCK, D_A), lambda b, i: (b, i, 0)),
        scratch_shapes=[pltpu.VMEM((GROUP_A, Q_BLOCK, 1), F32), pltpu.VMEM((GROUP_A, Q_BLOCK, 1), F32),
                        pltpu.VMEM((GROUP_A, Q_BLOCK, HEAD_DIM_A), F32)],
        compiler_params=_cparams(("parallel", "arbitrary")),
    )(z, z, z, ckt, cv, skt, sv, wkt, wv, e, ov)


def _nsa_sample_kernel(tab_ref, *refs, past, n_steps, t_new):
    pages = refs[:PAGES_PER_STEP]
    (q_ref, ckt_ref, cv_ref, new_ref, wkv_ref, g_ref, ga_ref, e_ref, ov_ref, o_ref, hist) = refs[PAGES_PER_STEP:]
    s = pl.program_id(1)
    for r in range(PAGES_PER_STEP):
        row = pl.multiple_of((s * PAGES_PER_STEP + r) * PAGE_SIZE, PAGE_SIZE)
        hist[pl.ds(row, PAGE_SIZE), :] = pages[r][...]

    @pl.when(s == 0)
    def _():
        hist[past:, :] = jnp.zeros((hist.shape[0] - past, KV_BRANCH), F32)
        hist[past:past + SUBLANES, :] = new_ref[...]

    @pl.when(s == n_steps - 1)
    def _():
        n_rows = N_HEADS_A * t_new
        n_keys = hist.shape[0]
        n_cmp = ckt_ref.shape[1]
        n_blk = ov_ref.shape[1]
        n_win = wkv_ref.shape[0]
        q = q_ref[...]
        row = lax.broadcasted_iota(jnp.int32, (n_rows, 1), 0)
        pos_q = past + row % t_new
        row_k = (row // t_new) % N_KV_A

        def diag(o_all):
            out = jnp.zeros((n_rows, HEAD_DIM_A), F32)
            for k in range(N_KV_A):
                out = out + jnp.where(row_k == k, o_all[:, k * HEAD_DIM_A:(k + 1) * HEAD_DIM_A], 0.0)
            return out

        cmp_end = lax.broadcasted_iota(jnp.int32, (1, n_cmp), 1) * CMP_STRIDE + (CMP_BLOCK - 1)
        s_c = jnp.dot(q, ckt_ref[...], preferred_element_type=F32)
        p = _masked_softmax(s_c, cmp_end <= pos_q).astype(BF16)
        o_cmp = diag(jnp.dot(p, cv_ref[...], preferred_element_type=F32))
        pj = jnp.dot(p, ov_ref[...], preferred_element_type=F32)
        rows_kq = N_KV_A * t_new
        p_slc = pj[0:rows_kq]
        for g in range(1, GROUP_A):
            p_slc = p_slc + pj[g * rows_kq:(g + 1) * rows_kq]

        j = lax.broadcasted_iota(jnp.int32, (rows_kq, n_blk), 1)
        pos_r = past + lax.broadcasted_iota(jnp.int32, (rows_kq, n_blk), 0) % t_new
        cur = pos_r // SEL_BLOCK
        valid = j * SEL_BLOCK <= pos_r
        forced = (j == 0) | (j == cur) | (j == cur - 1)
        score = jnp.where(valid, jnp.where(forced, FORCE_SCORE, p_slc), -jnp.inf)
        sel = _top_blocks(score, j, 1)
        sel = jnp.concatenate([sel] * GROUP_A, axis=0).astype(BF16)

        picked = jnp.dot(sel, e_ref[...], preferred_element_type=F32)
        hk = hist[:, 0:KV_W].astype(BF16)
        s_s = lax.dot_general(q, hk, (((1,), (1,)), ((), ())), preferred_element_type=F32)
        pos_k = lax.broadcasted_iota(jnp.int32, (1, n_keys), 1)
        p = _masked_softmax(s_s, (picked > 0.5) & (pos_k <= pos_q)).astype(BF16)
        o_sel = diag(jnp.dot(p, hist[:, KV_W:].astype(BF16), preferred_element_type=F32))

        wk = wkv_ref[:, 0:KV_W].astype(BF16)
        s_w = lax.dot_general(q, wk, (((1,), (1,)), ((), ())), preferred_element_type=F32)
        pos_w = past - WINDOW + lax.broadcasted_iota(jnp.int32, (1, n_win), 1)
        p = _masked_softmax(s_w, (pos_w <= pos_q) & (pos_q - pos_w < WINDOW)).astype(BF16)
        o_win = diag(jnp.dot(p, wkv_ref[:, KV_W:].astype(BF16), preferred_element_type=F32))

        gates = jax.nn.sigmoid(g_ref[...])
        o = gates[:, 0:1] * o_cmp + gates[:, 1:2] * o_sel + gates[:, 2:3] * o_win
        o_ref[...] = o * _silu(ga_ref[...])


def _nsa_sample(sel_view, table, qbs, ckt, cv, new_sel, wkv, g_rows, ga_rows, *, past, t_new):
    NB, n_pages = table.shape
    n_steps = n_pages // PAGES_PER_STEP
    n_keys = past + PAGE_SIZE
    n_cmp = ckt.shape[2]
    n_blk = 2 * LANES
    n_rows = qbs.shape[1]
    e = _block_expand(n_blk, n_keys)
    ov = _overlap(n_cmp, n_blk)

    def page_spec(r):
        return pl.BlockSpec((None, PAGE_SIZE, KV_BRANCH), lambda b, s, tab: (tab[b, s * PAGES_PER_STEP + r], 0, 0))

    def per_b(arr):
        return pl.BlockSpec((None,) + arr.shape[1:], lambda b, s, tab: (b, 0, 0))

    const2 = lambda b, s, tab: (0, 0)
    return pl.pallas_call(
        functools.partial(_nsa_sample_kernel, past=past, n_steps=n_steps, t_new=t_new),
        out_shape=jax.ShapeDtypeStruct((NB, n_rows, HEAD_DIM_A), F32),
        grid_spec=pltpu.PrefetchScalarGridSpec(
            num_scalar_prefetch=1,
            grid=(NB, n_steps),
            in_specs=[page_spec(r) for r in range(PAGES_PER_STEP)] + [
                per_b(qbs), per_b(ckt), per_b(cv), per_b(new_sel), per_b(wkv), per_b(g_rows), per_b(ga_rows),
                pl.BlockSpec(e.shape, const2, pipeline_mode=pl.Buffered(1)),
                pl.BlockSpec(ov.shape, const2, pipeline_mode=pl.Buffered(1))],
            out_specs=pl.BlockSpec((None, n_rows, HEAD_DIM_A), lambda b, s, tab: (b, 0, 0)),
            scratch_shapes=[pltpu.VMEM((n_keys, KV_BRANCH), F32)]),
        compiler_params=_cparams(("parallel", "arbitrary")),
    )(table, *([sel_view] * PAGES_PER_STEP), qbs, ckt, cv, new_sel, wkv, g_rows, ga_rows, e, ov)


def _permute_w_in_a(w):
    c_kv, c_g, c_ga, c_glu, c_gb = 1024, 2560, 2608, 3632, 5680
    cols = [w[:, :c_kv], w[:, c_ga:c_glu], w[:, c_glu:c_gb], w[:, c_gb:], w[:, c_kv:c_g], w[:, c_g:c_ga]]
    n = sum(c.shape[1] for c in cols)
    return jnp.concatenate(cols + [jnp.zeros((w.shape[0], EVEN_N - n), w.dtype)], axis=1).astype(BF16)


def _pad_rows(x, front, total):
    return jnp.pad(x, ((0, 0), (front, total - front - x.shape[1]), (0, 0)))


def _even_prompt(z, cmpw, conv_args, *, tq):
    B, T, _ = z.shape
    kv = z[:, :, COL_KV:COL_G]
    kv_cmp, kv_sel, kv_win = (kv[:, :, br * KV_BRANCH:(br + 1) * KV_BRANCH] for br in range(3))
    n_pages = T // PAGE_SIZE
    table = jnp.arange(B * n_pages, dtype=jnp.int32).reshape(B, n_pages)
    ckv = _compress(kv_cmp.reshape(B * n_pages, SUBLANES, CMP_STRIDE * KV_BRANCH), table, *cmpw)

    def kt(x):
        return jnp.swapaxes(x, 1, 2).astype(BF16)

    o_a = _nsa_prompt(z, kt(ckv[:, :, :KV_W]), ckv[:, :, KV_W:].astype(BF16),
                      kt(kv_sel[:, :, :KV_W]), kv_sel[:, :, KV_W:].astype(BF16),
                      kt(kv_win[:, :, :KV_W]), kv_win[:, :, KV_W:].astype(BF16))
    u = _glu(z, tt=tq)
    halo_map = lambda bb, i: (bb, jnp.maximum(i * (tq // CONV_HALO) - 1, 0), 0)
    o_b = _conv(u, halo_map, True, u, z, COL_GB // D_MODEL, *conv_args, tt=tq)
    kv_shape = (B, T, 2, N_KV_A, HEAD_DIM_A)
    w_keep = min(WINDOW, T)
    outs = (kv_cmp.reshape(kv_shape), kv_sel.reshape(kv_shape),
            kv_win[:, T - w_keep:].reshape(B, w_keep, 2, N_KV_A, HEAD_DIM_A), u[:, T - (CONV_WIDTH - 1):])
    return o_a, o_b, outs


def _even_sample(z, cmpw, conv_args, cmp_view, sel_view, table, win_buf, conv_hist, *, past):
    NB, t_new, _ = z.shape
    kv = z[:, :, COL_KV:COL_G]
    kv_cmp, kv_sel, kv_win = (kv[:, :, br * KV_BRANCH:(br + 1) * KV_BRANCH] for br in range(3))
    ckv = _compress(cmp_view, table, *cmpw)
    q = z[:, :, COL_Q:COL_Q + D_A].reshape(NB, t_new, N_KV_A, GROUP_A, HEAD_DIM_A) * (HEAD_DIM_A ** -0.5)
    qbs = jnp.einsum('bqkgd,kK->bgkqKd', q, jnp.eye(N_KV_A, dtype=F32)).reshape(NB, N_HEADS_A * t_new, KV_W)
    g_rows = z[:, :, COL_G:COL_G + 3 * N_HEADS_A].reshape(NB, t_new, 3, N_KV_A, GROUP_A)
    g_rows = jnp.transpose(g_rows, (0, 4, 3, 1, 2)).reshape(NB, N_HEADS_A * t_new, 3)
    g_rows = jnp.pad(g_rows, ((0, 0), (0, 0), (0, LANES - 3)))
    ga_rows = z[:, :, COL_GA:COL_GA + D_A].reshape(NB, t_new, N_KV_A, GROUP_A, HEAD_DIM_A)
    ga_rows = jnp.transpose(ga_rows, (0, 3, 2, 1, 4)).reshape(NB, N_HEADS_A * t_new, HEAD_DIM_A)
    wkv = jnp.concatenate([win_buf.reshape(NB, -1, KV_BRANCH), kv_win], axis=1)
    wb = win_buf.shape[1]
    o = _nsa_sample(sel_view, table, qbs.astype(BF16),
                    jnp.swapaxes(ckv[:, :, :KV_W], 1, 2).astype(BF16), ckv[:, :, KV_W:].astype(BF16),
                    _pad_rows(kv_sel, 0, SUBLANES), _pad_rows(wkv, 0, WIN_KEYS), g_rows, ga_rows,
                    past=past, t_new=t_new)
    o_a = jnp.transpose(o.reshape(NB, GROUP_A, N_KV_A, t_new, HEAD_DIM_A), (0, 3, 2, 1, 4))
    o_a = o_a.reshape(NB * t_new, D_A).astype(BF16)
    u = _glu(z.reshape(1, NB * t_new, EVEN_N), tt=NB * t_new).reshape(NB, t_new, D_MODEL)
    halo = _pad_rows(conv_hist, CONV_HALO - (CONV_WIDTH - 1), CONV_HALO)
    gate_b = _pad_rows(z[:, :, COL_GB:COL_GB + D_MODEL], 0, SUBLANES)
    o_b = _conv(halo, lambda bb, i: (bb, 0, 0), False, _pad_rows(u, 0, SUBLANES), gate_b, 0, *conv_args, tt=SUBLANES)
    o_b = o_b[:, :t_new].reshape(NB * t_new, D_MODEL)
    kv_shape = (NB, t_new, 2, N_KV_A, HEAD_DIM_A)
    outs = (kv_cmp.reshape(kv_shape), kv_sel.reshape(kv_shape),
            wkv[:, t_new:].reshape(NB, wb, 2, N_KV_A, HEAD_DIM_A),
            jnp.concatenate([conv_hist, u], axis=1)[:, -(CONV_WIDTH - 1):])
    return o_a, o_b, outs


def kernel(x_prompt, x_sample, mem_prompt, cache_cmp_kv, cache_sel_kv, cache_win_kv, state_conv, state_pool,
           cache_mem_kv, page_table, norm_mix, norm_mem, norm_out, w_in_a, w_out_a, cmp_pe, cmp_w1, cmp_w2,
           conv_w, conv_b, conv_ln_g, conv_ln_b, w_in_c, pool_w, pool_scale, w_out_c, w_mem_q, w_mem_kv, w_mem_o):
    B, T, D = x_prompt.shape
    NB, t_new, _ = x_sample.shape
    depth = norm_mix.shape[0]
    n_pool = cache_cmp_kv.shape[1]
    past = page_table.shape[1] * PAGE_SIZE
    mem_len = mem_prompt.shape[1]
    tm_p = 1024
    tq = 256
    ms = NB * t_new
    xp = x_prompt.reshape(B * T, D)
    xs = x_sample.reshape(ms, D)
    p_cmp, p_sel, p_win, p_conv, p_pool, p_mem = [], [], [], [], [], []
    s_cmp, s_sel, s_win, s_conv, s_pool = [], [], [], [], []
    cmp_view = cache_cmp_kv.reshape(-1, SUBLANES, CMP_STRIDE * KV_BRANCH)
    sel_view = cache_sel_kv.reshape(-1, PAGE_SIZE, KV_BRANCH)

    for layer in range(depth):
        if layer % 2 == 0:
            a = layer // 2
            w_in = _permute_w_in_a(w_in_a[a])
            w_out = w_out_a[a].astype(BF16)
            cmpw = _compress_weights(cmp_pe[a], cmp_w1[a], cmp_w2[a])
            conv_args = (conv_w[a], conv_b[a], conv_ln_g[a], conv_ln_b[a])
            zp = _mm([xp], w_in, gain=norm_mix[layer], tm=tm_p, tn=EVEN_TN).reshape(B, T, EVEN_N)
            o_a, o_b, outs = _even_prompt(zp, cmpw, conv_args, tq=tq)
            xp = _mm([o_a.reshape(B * T, D_A), o_b.reshape(B * T, D)], w_out, res=xp, tm=tm_p, tn=D)
            for lst, val in zip((p_cmp, p_sel, p_win, p_conv), outs):
                lst.append(val)
            zs = _mm([xs], w_in, gain=norm_mix[layer], tm=ms, tn=EVEN_TN).reshape(NB, t_new, EVEN_N)
            table = page_table + a * n_pool
            o_a, o_b, outs = _even_sample(zs, cmpw, conv_args, cmp_view, sel_view, table, cache_win_kv[a],
                                          state_conv[a], past=past)
            xs = _mm([o_a, o_b], w_out, res=xs, tm=ms, tn=D)
            for lst, val in zip((s_cmp, s_sel, s_win, s_conv), outs):
                lst.append(val)
        else:
            c = layer // 2
            w_in = w_in_c[c].astype(BF16)
            w_out = w_out_c[c].astype(BF16)
            zp = _mm([xp], w_in, gain=norm_mix[layer], tm=tm_p, tn=D).reshape(B, T, 2 * D)
            halo_map = lambda bb, i: (bb, jnp.maximum(i * (tq // POOL_HALO) - 1, 0), 0)
            m = _pool(zp, halo_map, True, zp, pool_w[c], pool_scale[c], tt=tq, pos0=0)
            xp = _mm([m.reshape(B * T, D)], w_out, res=xp, tm=tm_p, tn=D)
            p_pool.append(zp[:, T - POOL_HIST:, :D])
            zs = _mm([xs], w_in, gain=norm_mix[layer], tm=ms, tn=D).reshape(NB, t_new, 2 * D)
            hist = state_pool[c]
            m = _pool(_pad_rows(hist, POOL_HALO - POOL_HIST, POOL_HALO), lambda bb, i: (bb, 0, 0), False,
                      _pad_rows(zs, 0, SUBLANES), pool_w[c], pool_scale[c], tt=SUBLANES, pos0=past)
            xs = _mm([m[:, :t_new].reshape(ms, D)], w_out, res=xs, tm=ms, tn=D)
            s_pool.append(jnp.concatenate([hist, zs[:, :, :D]], axis=1)[:, -POOL_HIST:])

        wq = w_mem_q[layer].astype(BF16)
        wo = w_mem_o[layer].astype(BF16)
        mkv = _mm([mem_prompt.reshape(B * mem_len, D)], w_mem_kv[layer].astype(BF16), tm=B * mem_len, tn=D)
        mkv = mkv.reshape(B, mem_len, 2 * D)
        p_mem.append(mkv.reshape(B, mem_len, 2, N_HEADS_M, HEAD_DIM_M))
        q = _mm([xp], wq, gain=norm_mem[layer], tm=tm_p, tn=D, out_dtype=BF16).reshape(B, T, D)
        o = _mem_attn(q, jnp.swapaxes(mkv[:, :, :D], 1, 2).astype(BF16), mkv[:, :, D:].astype(BF16), tm=tq)
        xp = _mm([o.reshape(B * T, D)], wo, res=xp, tm=tm_p, tn=D)
        ckv = cache_mem_kv[layer].reshape(NB, mem_len, 2 * D)
        q = _mm([xs], wq, gain=norm_mem[layer], tm=ms, tn=D, out_dtype=BF16).reshape(NB, t_new, D)
        o = _mem_attn(_pad_rows(q, 0, SUBLANES), jnp.swapaxes(ckv[:, :, :D], 1, 2).astype(BF16),
                      ckv[:, :, D:].astype(BF16), tm=SUBLANES)
        xs = _mm([o[:, :t_new].reshape(ms, D)], wo, res=xs, tm=ms, tn=D)

    y_prompt = _rmsnorm(xp, norm_out, tm=tm_p).reshape(B, T, D)
    y_sample = _rmsnorm(xs, norm_out, tm=ms).reshape(NB, t_new, D)
    return (y_prompt, y_sample,
            jnp.stack(p_cmp), jnp.stack(p_sel), jnp.stack(p_win), jnp.stack(p_conv), jnp.stack(p_pool),
            jnp.stack(p_mem),
            jnp.stack(s_cmp), jnp.stack(s_sel), jnp.stack(s_win), jnp.stack(s_conv), jnp.stack(s_pool))
```

```python
import functools

import jax
import jax.numpy as jnp
from jax import lax
from jax.experimental import pallas as pl
from jax.experimental.pallas import tpu as pltpu

F32 = jnp.float32
BF16 = jnp.bfloat16

LANES = 128
SUBLANES = 8
VMEM_LIMIT_BYTES = 56 * 1024 * 1024

D_MODEL = 1024
N_HEADS_A = 16
HEAD_DIM_A = 64
N_KV_A = 4
GROUP_A = N_HEADS_A // N_KV_A
D_A = N_HEADS_A * HEAD_DIM_A
KV_W = N_KV_A * HEAD_DIM_A
KV_BRANCH = 2 * KV_W
CMP_STRIDE = 16
CMP_BLOCK = 32
CMP_HIDDEN = 128
SEL_BLOCK = 64
N_SEL = 16
WINDOW = 512
Q_BLOCK = 128
FORCE_SCORE = 1000.0
CONV_WIDTH = 31
CONV_HALO = 32
POOL_WINDOWS = (2, 4, 8, 16)
POOL_GROUP = D_MODEL // len(POOL_WINDOWS)
POOL_HIST = 15
POOL_HALO = 16
PAGE_SIZE = 128
N_HEADS_M = 4
HEAD_DIM_M = 256
NORM_EPS = 1e-6
NEG = -1e30

COL_Q, COL_GA, COL_VAL, COL_GL, COL_GB = 0, 1024, 2048, 3072, 4096
COL_KV = 5120
COL_G = COL_KV + 3 * KV_BRANCH
EVEN_N = 6912
EVEN_TN = 768
PAGES_PER_STEP = 8


def _cparams(sem):
    return pltpu.CompilerParams(dimension_semantics=sem, vmem_limit_bytes=VMEM_LIMIT_BYTES)


def _silu(x):
    return x * jax.nn.sigmoid(x)


def _mm_kernel(*refs, n_parts, has_gain, has_res):
    parts = refs[:n_parts]
    pos = n_parts
    gain_ref = refs[pos] if has_gain else None
    pos += int(has_gain)
    w_ref = refs[pos]
    pos += 1
    res_ref = refs[pos] if has_res else None
    pos += int(has_res)
    o_ref, lhs = refs[pos], refs[pos + 1]

    @pl.when(pl.program_id(1) == 0)
    def _():
        off = 0
        for p in parts:
            x = p[...].astype(F32)
            if has_gain:
                x = x * lax.rsqrt(jnp.mean(x * x, axis=-1, keepdims=True) + NORM_EPS) * gain_ref[...]
            lhs[:, off:off + p.shape[1]] = x.astype(BF16)
            off += p.shape[1]

    acc = jnp.dot(lhs[...], w_ref[...], preferred_element_type=F32)
    if has_res:
        acc = acc + res_ref[...]
    o_ref[...] = acc.astype(o_ref.dtype)


def _mm(parts, w, *, gain=None, res=None, tm, tn, out_dtype=F32):
    M = parts[0].shape[0]
    K, N = w.shape
    assert sum(p.shape[1] for p in parts) == K and M % tm == 0 and N % tn == 0
    in_specs = [pl.BlockSpec((tm, p.shape[1]), lambda i, j: (i, 0)) for p in parts]
    args = list(parts)
    if gain is not None:
        in_specs.append(pl.BlockSpec((1, K), lambda i, j: (0, 0)))
        args.append(gain.reshape(1, K).astype(F32))
    in_specs.append(pl.BlockSpec((K, tn), lambda i, j: (0, j)))
    args.append(w)
    if res is not None:
        in_specs.append(pl.BlockSpec((tm, tn), lambda i, j: (i, j)))
        args.append(res)
    return pl.pallas_call(
        functools.partial(_mm_kernel, n_parts=len(parts), has_gain=gain is not None, has_res=res is not None),
        out_shape=jax.ShapeDtypeStruct((M, N), out_dtype),
        grid=(M // tm, N // tn),
        in_specs=in_specs,
        out_specs=pl.BlockSpec((tm, tn), lambda i, j: (i, j)),
        scratch_shapes=[pltpu.VMEM((tm, K), BF16)],
        compiler_params=_cparams(("parallel", "arbitrary")),
        name="mm",
    )(*args)


def _rmsnorm_kernel(x_ref, g_ref, o_ref):
    x = x_ref[...]
    o_ref[...] = x * lax.rsqrt(jnp.mean(x * x, axis=-1, keepdims=True) + NORM_EPS) * g_ref[...]


def _rmsnorm(x, g, *, tm):
    M, D = x.shape
    return pl.pallas_call(
        _rmsnorm_kernel,
        out_shape=jax.ShapeDtypeStruct((M, D), F32),
        grid=(M // tm,),
        in_specs=[pl.BlockSpec((tm, D), lambda i: (i, 0)), pl.BlockSpec((1, D), lambda i: (0, 0))],
        out_specs=pl.BlockSpec((tm, D), lambda i: (i, 0)),
        compiler_params=_cparams(("parallel",)),
        name="rmsnorm",
    )(x, g.reshape(1, D))


def _glu_kernel(val_ref, gl_ref, o_ref):
    o_ref[...] = val_ref[...] * jax.nn.sigmoid(gl_ref[...])


def _glu(z, *, tt):
    B, T, _ = z.shape
    blk = (None, tt, D_MODEL)
    return pl.pallas_call(
        _glu_kernel,
        out_shape=jax.ShapeDtypeStruct((B, T, D_MODEL), F32),
        grid=(B, T // tt),
        in_specs=[pl.BlockSpec(blk, lambda b, i: (b, i, COL_VAL // D_MODEL)),
                  pl.BlockSpec(blk, lambda b, i: (b, i, COL_GL // D_MODEL))],
        out_specs=pl.BlockSpec(blk, lambda b, i: (b, i, 0)),
        compiler_params=_cparams(("parallel", "parallel")),
        name="glu",
    )(z, z)


CONV_ROWS = 32


def _conv_kernel(halo_ref, cur_ref, gate_ref, w_ref, b_ref, lg_ref, lb_ref, o_ref, buf, y_scr, *, tt, zero_first_halo):
    halo = halo_ref[...]
    if zero_first_halo:
        halo = jnp.where(pl.program_id(1) == 0, 0.0, halo)
    buf[0:CONV_HALO, :] = halo
    buf[CONV_HALO:CONV_HALO + tt, :] = cur_ref[...]
    first = CONV_HALO - (CONV_WIDTH - 1)
    rc = min(CONV_ROWS, tt)
    for r0 in range(0, tt, rc):
        acc = jnp.broadcast_to(b_ref[...], (rc, D_MODEL))
        for j in range(CONV_WIDTH):
            acc = acc + w_ref[j:j + 1, :] * buf[first + r0 + j:first + r0 + j + rc, :]
        y_scr[r0:r0 + rc, :] = acc
    y = y_scr[...]
    xc = y - jnp.mean(y, axis=-1, keepdims=True)
    yn = xc * lax.rsqrt(jnp.mean(xc * xc, axis=-1, keepdims=True) + NORM_EPS) * lg_ref[...] + lb_ref[...]
    o_ref[...] = (_silu(yn) * _silu(gate_ref[...])).astype(o_ref.dtype)


def _conv(halo_arr, halo_map, zero_first_halo, u, gate_arr, gate_col, w, b, lg, lb, *, tt):
    B, T, _ = u.shape
    w32 = jnp.concatenate([w, jnp.zeros((CONV_HALO - CONV_WIDTH + 1, D_MODEL), F32)], axis=0)[:CONV_HALO]
    vec = pl.BlockSpec((1, D_MODEL), lambda bb, i: (0, 0))
    return pl.pallas_call(
        functools.partial(_conv_kernel, tt=tt, zero_first_halo=zero_first_halo),
        out_shape=jax.ShapeDtypeStruct((B, T, D_MODEL), BF16),
        grid=(B, T // tt),
        in_specs=[pl.BlockSpec((None, CONV_HALO, D_MODEL), halo_map),
                  pl.BlockSpec((None, tt, D_MODEL), lambda bb, i: (bb, i, 0)),
                  pl.BlockSpec((None, tt, D_MODEL), lambda bb, i: (bb, i, gate_col)),
                  pl.BlockSpec((CONV_HALO, D_MODEL), lambda bb, i: (0, 0)),
                  vec, vec, vec],
        out_specs=pl.BlockSpec((None, tt, D_MODEL), lambda bb, i: (bb, i, 0)),
        scratch_shapes=[pltpu.VMEM((CONV_HALO + tt, D_MODEL), F32), pltpu.VMEM((tt, D_MODEL), F32)],
        compiler_params=_cparams(("parallel", "parallel")),
        name="conv",
    )(halo_arr, u, gate_arr, w32, b.reshape(1, -1), lg.reshape(1, -1), lb.reshape(1, -1))


def _pool_kernel(halo_ref, cur_ref, gate_ref, pw_ref, ps_ref, o_ref, buf, *, tt, zero_first_halo, pos0):
    i = pl.program_id(1)
    halo = halo_ref[...]
    if zero_first_halo:
        halo = jnp.where(i == 0, 0.0, halo)
    buf[0:POOL_HALO, :] = halo
    buf[POOL_HALO:POOL_HALO + tt, :] = cur_ref[...]
    pos = pos0 + i * tt + lax.broadcasted_iota(jnp.int32, (tt, 1), 0)
    for g, w in enumerate(POOL_WINDOWS):
        c0, c1 = g * POOL_GROUP, (g + 1) * POOL_GROUP
        s = buf[POOL_HALO:POOL_HALO + tt, c0:c1]
        for j in range(1, w):
            s = s + buf[POOL_HALO - j:POOL_HALO - j + tt, c0:c1]
        cnt = jnp.minimum(w, pos + 1).astype(F32)
        d = s / cnt - buf[POOL_HALO:POOL_HALO + tt, c0:c1]
        m = jnp.dot(d.astype(BF16), pw_ref[g], preferred_element_type=F32)
        o_ref[:, c0:c1] = (m * ps_ref[:, c0:c1] * _silu(gate_ref[:, c0:c1])).astype(o_ref.dtype)


def _pool(halo_arr, halo_map, zero_first_halo, z, pool_w, pool_scale, *, tt, pos0):
    B, T, _ = z.shape
    return pl.pallas_call(
        functools.partial(_pool_kernel, tt=tt, zero_first_halo=zero_first_halo, pos0=pos0),
        out_shape=jax.ShapeDtypeStruct((B, T, D_MODEL), BF16),
        grid=(B, T // tt),
        in_specs=[pl.BlockSpec((None, POOL_HALO, D_MODEL), halo_map),
                  pl.BlockSpec((None, tt, D_MODEL), lambda bb, i: (bb, i, 0)),
                  pl.BlockSpec((None, tt, D_MODEL), lambda bb, i: (bb, i, 1)),
                  pl.BlockSpec((len(POOL_WINDOWS), POOL_GROUP, POOL_GROUP), lambda bb, i: (0, 0, 0)),
                  pl.BlockSpec((1, D_MODEL), lambda bb, i: (0, 0))],
        out_specs=pl.BlockSpec((None, tt, D_MODEL), lambda bb, i: (bb, i, 0)),
        scratch_shapes=[pltpu.VMEM((POOL_HALO + tt, D_MODEL), F32)],
        compiler_params=_cparams(("parallel", "parallel")),
        name="pool",
    )(halo_arr, z, z, pool_w.astype(BF16), pool_scale.reshape(1, -1))


def _mem_attn_kernel(q_ref, kt_ref, v_ref, o_ref):
    for h in range(N_HEADS_M):
        c0, c1 = h * HEAD_DIM_M, (h + 1) * HEAD_DIM_M
        q = q_ref[:, c0:c1] * (HEAD_DIM_M ** -0.5)
        s = jnp.dot(q, kt_ref[c0:c1, :], preferred_element_type=F32)
        e = jnp.exp(s - jnp.max(s, axis=-1, keepdims=True))
        p = e / jnp.sum(e, axis=-1, keepdims=True)
        o = jnp.dot(p.astype(BF16), v_ref[:, c0:c1], preferred_element_type=F32)
        o_ref[:, c0:c1] = o.astype(o_ref.dtype)


def _mem_attn(q, kt, v, *, tm):
    NB, Tq, D = q.shape
    M = v.shape[1]
    return pl.pallas_call(
        _mem_attn_kernel,
        out_shape=jax.ShapeDtypeStruct((NB, Tq, D), BF16),
        grid=(NB, Tq // tm),
        in_specs=[pl.BlockSpec((None, tm, D), lambda b, i: (b, i, 0)),
                  pl.BlockSpec((None, D, M), lambda b, i: (b, 0, 0)),
                  pl.BlockSpec((None, M, D), lambda b, i: (b, 0, 0))],
        out_specs=pl.BlockSpec((None, tm, D), lambda b, i: (b, i, 0)),
        compiler_params=_cparams(("parallel", "parallel")),
        name="mem_attn",
    )(q, kt, v)


def _compress_weights(pe, w1, w2):
    eye2 = jnp.eye(2, dtype=F32)

    def bd(wc):
        return jnp.einsum('ldh,kK->lkdKh', wc, eye2).reshape(CMP_STRIDE * LANES, 2 * CMP_HIDDEN)

    wfs = jnp.stack([jnp.concatenate([bd(w1[c, :CMP_STRIDE]), bd(w1[c, CMP_STRIDE:])], axis=1)
                     for c in range(2) for _ in range(2)]).astype(BF16)
    w2bd = jnp.stack([jnp.einsum('hd,kK->khKd', w2[c], eye2).reshape(2 * CMP_HIDDEN, LANES)
                      for c in range(2)]).astype(BF16)
    pe_rows = jnp.zeros((2, 2 * SUBLANES, CMP_BLOCK * HEAD_DIM_A), F32).at[:, 0].set(pe.reshape(2, -1))
    bias = jnp.stack([_mm([pe_rows[c]], w1[c].reshape(CMP_BLOCK * HEAD_DIM_A, CMP_HIDDEN).astype(BF16),
                          tm=2 * SUBLANES, tn=CMP_HIDDEN)[0] for c in range(2)])
    bias2 = jnp.concatenate([bias, bias], axis=1).reshape(2, 1, 2 * CMP_HIDDEN)
    return wfs, w2bd, bias2


def _compress_kernel(tab_ref, *refs, n_rows, n_steps):
    pages = refs[:PAGES_PER_STEP]
    wfs_ref, bias_ref, w2_ref, o_ref, xs, sec = refs[PAGES_PER_STEP:]
    s = pl.program_id(1)
    for r in range(PAGES_PER_STEP):
        row = pl.multiple_of((s * PAGES_PER_STEP + r) * SUBLANES, SUBLANES)
        xs[pl.ds(row, SUBLANES), :] = pages[r][...]

    @pl.when(s == 0)
    def _():
        xs[n_rows:n_rows + SUBLANES, :] = jnp.zeros((SUBLANES, xs.shape[1]), F32)

    @pl.when(s == n_steps - 1)
    def _():
        row_w = CMP_STRIDE * KV_BRANCH
        for grp in range(4):
            c, kp = divmod(grp, 2)
            base = c * KV_W + kp * LANES
            xg = jnp.concatenate([xs[:, l * KV_BRANCH + base:l * KV_BRANCH + base + LANES]
                                  for l in range(CMP_STRIDE)], axis=1).astype(BF16)
            assert xg.shape[1] * 4 == row_w
            fs = jnp.dot(xg, wfs_ref[grp], preferred_element_type=F32)
            sec[...] = fs[:, 2 * CMP_HIDDEN:]
            hid = _silu(fs[0:n_rows, :2 * CMP_HIDDEN] + sec[1:n_rows + 1, :] + bias_ref[c])
            o_ref[:, base:base + LANES] = jnp.dot(hid.astype(BF16), w2_ref[c], preferred_element_type=F32)


def _compress(cache_view, table, wfs, w2bd, bias2):
    NB, n_pages = table.shape
    n_steps = n_pages // PAGES_PER_STEP
    n_rows = n_pages * SUBLANES
    row_w = cache_view.shape[2]

    def page_spec(r):
        return pl.BlockSpec((None, SUBLANES, row_w), lambda b, s, tab: (tab[b, s * PAGES_PER_STEP + r], 0, 0))

    const3 = lambda b, s, tab: (0, 0, 0)
    return pl.pallas_call(
        functools.partial(_compress_kernel, n_rows=n_rows, n_steps=n_steps),
        out_shape=jax.ShapeDtypeStruct((NB, n_rows, KV_BRANCH), F32),
        grid_spec=pltpu.PrefetchScalarGridSpec(
            num_scalar_prefetch=1,
            grid=(NB, n_steps),
            in_specs=[page_spec(r) for r in range(PAGES_PER_STEP)] + [
                pl.BlockSpec(wfs.shape, const3), pl.BlockSpec(bias2.shape, const3), pl.BlockSpec(w2bd.shape, const3)],
            out_specs=pl.BlockSpec((None, n_rows, KV_BRANCH), lambda b, s, tab: (b, 0, 0)),
            scratch_shapes=[pltpu.VMEM((n_rows + SUBLANES, row_w), F32),
                            pltpu.VMEM((n_rows + SUBLANES, 2 * CMP_HIDDEN), F32)]),
        compiler_params=_cparams(("parallel", "arbitrary")),
        name="compress",
    )(table, *([cache_view] * PAGES_PER_STEP), wfs, bias2, w2bd)


def _overlap(n_cmp_rows, n_sel_cols):
    c0 = jnp.arange(n_cmp_rows) * CMP_STRIDE
    s0 = jnp.arange(n_sel_cols) * SEL_BLOCK
    ov = jnp.minimum(c0[:, None] + CMP_BLOCK, s0[None, :] + SEL_BLOCK) - jnp.maximum(c0[:, None], s0[None, :])
    return (jnp.clip(ov, 0, None).astype(F32) / CMP_BLOCK).astype(BF16)


def _block_expand(n_blocks, n_keys):
    return (jnp.arange(n_keys)[None, :] // SEL_BLOCK == jnp.arange(n_blocks)[:, None]).astype(BF16)


def _masked_softmax(s, mask, axis=-1):
    m = jnp.max(jnp.where(mask, s, NEG), axis=axis, keepdims=True)
    e = jnp.where(mask, jnp.exp(s - m), 0.0)
    d = jnp.sum(e, axis=axis, keepdims=True)
    return e / jnp.where(d > 0, d, 1.0)


def _top_blocks(score, idx, axis):
    n = score.shape[axis]
    sel = jnp.zeros(score.shape, F32)
    work = score
    for _ in range(N_SEL):
        m = jnp.max(work, axis=axis, keepdims=True)
        first = jnp.min(jnp.where(work == m, idx, n), axis=axis, keepdims=True)
        pick = idx == first
        sel = jnp.where(pick & (m > -jnp.inf), 1.0, sel)
        work = jnp.where(pick, -jnp.inf, work)
    return sel


SEL_TK = 512
WIN_KEYS = WINDOW + Q_BLOCK
LOG2E = 1.4426950408889634
V_ROWS = HEAD_DIM_A + 16


def _values_t(v):
    B, n, _ = v.shape
    vt = jnp.swapaxes(v, 1, 2).reshape(B, N_KV_A, HEAD_DIM_A, n)
    pad = jnp.zeros((B, N_KV_A, V_ROWS - HEAD_DIM_A, n), F32).at[:, :, 0].set(1.0)
    return jnp.concatenate([vt, pad], axis=2).reshape(B, N_KV_A * V_ROWS, n).astype(BF16)


def _nsa_prompt_kernel(q_ref, ga_ref, g_ref, ck_ref, cvt_ref, sk_ref, svt_ref, wk_ref, wvt_ref, et_ref, ovt_ref,
                       o_ref, qt_scr, ot_scr, qb_scr, m_scr, acc_scr):
    i = pl.program_id(1)
    q0 = i * Q_BLOCK
    n_cmp = ck_ref.shape[0]
    pos_q = q0 + lax.broadcasted_iota(jnp.int32, (1, Q_BLOCK), 1)
    for c in range(D_A // LANES):
        cols = slice(c * LANES, (c + 1) * LANES)
        qt_scr[cols, :] = (q_ref[:, cols].T * (HEAD_DIM_A ** -0.5 * LOG2E)).astype(BF16)
    gates_t = jax.nn.sigmoid(g_ref[...]).T
    j_t = lax.broadcasted_iota(jnp.int32, (LANES, Q_BLOCK), 0)
    cur_t = pos_q // SEL_BLOCK
    valid_t = j_t * SEL_BLOCK <= pos_q
    forced_t = (j_t == 0) | (j_t == cur_t) | (j_t == cur_t - 1)
    cmp_end = lax.broadcasted_iota(jnp.int32, (n_cmp, 1), 0) * CMP_STRIDE + (CMP_BLOCK - 1)
    cbias = jnp.where(cmp_end <= pos_q, 0.0, NEG)
    cvalid = jnp.where(pos_q >= CMP_BLOCK - 1, 1.0, 0.0)
    w0 = pl.multiple_of(jnp.maximum(q0 - WINDOW, 0), Q_BLOCK)
    pos_w = w0 + lax.broadcasted_iota(jnp.int32, (WIN_KEYS, 1), 0)
    wbias = jnp.where((pos_w <= pos_q) & (pos_q - pos_w < WINDOW), 0.0, NEG)
    n_full = q0 // SEL_TK
    zero_half = jnp.zeros((HEAD_DIM_A, Q_BLOCK), BF16)

    def weights_t(s, bias):
        out = []
        for g in range(GROUP_A):
            sg = s[:, g * Q_BLOCK:(g + 1) * Q_BLOCK] + bias
            out.append(jnp.exp2(sg - jnp.max(sg, axis=0, keepdims=True)).astype(BF16))
        return jnp.concatenate(out, axis=1)

    def normalise(o_ext, scale=None):
        d = o_ext[HEAD_DIM_A:HEAD_DIM_A + 1, :]
        inv = 1.0 / jnp.where(d > 0, d, 1.0)
        return inv if scale is None else inv * scale

    for k in range(N_KV_A):
        vr = slice(k * V_ROWS, (k + 1) * V_ROWS)
        pr = slice((k // 2) * LANES, (k // 2 + 1) * LANES)
        blocks = []
        for g in range(GROUP_A):
            h = k * GROUP_A + g
            qh = qt_scr[h * HEAD_DIM_A:(h + 1) * HEAD_DIM_A, :]
            blocks.append(jnp.concatenate([qh, zero_half] if k % 2 == 0 else [zero_half, qh], axis=0))
        qt = jnp.concatenate(blocks, axis=1)

        s = jnp.dot(ck_ref[:, pr], qt, preferred_element_type=F32)
        e = weights_t(s, cbias)
        o_ext = jnp.dot(cvt_ref[vr, :], e, preferred_element_type=F32)
        inv = normalise(o_ext, jnp.concatenate([cvalid] * GROUP_A, axis=1))
        o_cmp = o_ext[:HEAD_DIM_A] * inv
        pj = jnp.dot(ovt_ref[...], e, preferred_element_type=F32) * inv
        p_slc = pj[:, 0:Q_BLOCK]
        for g in range(1, GROUP_A):
            p_slc = p_slc + pj[:, g * Q_BLOCK:(g + 1) * Q_BLOCK]

        score_t = jnp.where(valid_t, jnp.where(forced_t, FORCE_SCORE, p_slc), -jnp.inf)
        sel_bias = ((_top_blocks(score_t, j_t, 0) - 1.0) * (-NEG)).astype(BF16)
        qb = jnp.concatenate([qt, jnp.concatenate([sel_bias] * GROUP_A, axis=1)], axis=0)

        qb_scr[k] = qb

        s = jnp.dot(wk_ref[pl.ds(w0, WIN_KEYS), pr], qt, preferred_element_type=F32)
        o_ext = jnp.dot(wvt_ref[vr, pl.ds(w0, WIN_KEYS)], weights_t(s, wbias), preferred_element_type=F32)
        o_win = o_ext[:HEAD_DIM_A] * normalise(o_ext)

        for g in range(GROUP_A):
            h = k * GROUP_A + g
            lanes = slice(g * Q_BLOCK, (g + 1) * Q_BLOCK)
            ot_scr[h * HEAD_DIM_A:(h + 1) * HEAD_DIM_A, :] = (
                gates_t[h:h + 1, :] * o_cmp[:, lanes]
                + gates_t[2 * N_HEADS_A + h:2 * N_HEADS_A + h + 1, :] * o_win[:, lanes])

    m_scr[...] = jnp.full(m_scr.shape, NEG, F32)
    acc_scr[...] = jnp.zeros(acc_scr.shape, F32)

    def sel_step(t, causal):
        k0 = pl.multiple_of(t * SEL_TK, SEL_TK)
        onehot = et_ref[pl.ds(k0, SEL_TK), :]
        if causal:
            pos_k = k0 + lax.broadcasted_iota(jnp.int32, (SEL_TK, 1), 0)
            future = jnp.where(pos_k <= pos_q, 0.0, NEG)

        def scores(k):
            pr = slice((k // 2) * LANES, (k // 2 + 1) * LANES)
            kb = jnp.concatenate([sk_ref[pl.ds(k0, SEL_TK), pr], onehot], axis=1)
            return jnp.dot(kb, qb_scr[k], preferred_element_type=F32)

        s_next = scores(0)
        for k in range(N_KV_A):
            s = s_next
            if k + 1 < N_KV_A:
                s_next = scores(k + 1)
            es, alphas = [], []
            for g in range(GROUP_A):
                lanes = slice(g * Q_BLOCK, (g + 1) * Q_BLOCK)
                sg = s[:, lanes] + future if causal else s[:, lanes]
                m_old = m_scr[k, :, lanes]
                m_new = jnp.maximum(m_old, jnp.max(sg, axis=0, keepdims=True))
                alphas.append(jnp.exp2(m_old - m_new))
                es.append(jnp.exp2(sg - m_new).astype(BF16))
                m_scr[k, :, lanes] = m_new
            pv = jnp.dot(svt_ref[k * V_ROWS:(k + 1) * V_ROWS, pl.ds(k0, SEL_TK)], jnp.concatenate(es, axis=1),
                         preferred_element_type=F32)
            acc_scr[k] = jnp.concatenate(alphas, axis=1) * acc_scr[k] + pv

    def full_step(t, carry):
        sel_step(t, False)
        return carry

    lax.fori_loop(0, n_full, full_step, 0)
    sel_step(n_full, True)
    for k in range(N_KV_A):
        acc = acc_scr[k]
        o_sel = acc[:HEAD_DIM_A] * normalise(acc)
        for g in range(GROUP_A):
            h = k * GROUP_A + g
            lanes = slice(g * Q_BLOCK, (g + 1) * Q_BLOCK)
            rows = slice(h * HEAD_DIM_A, (h + 1) * HEAD_DIM_A)
            ot_scr[rows, :] = ot_scr[rows, :] + gates_t[N_HEADS_A + h:N_HEADS_A + h + 1, :] * o_sel[:, lanes]

    for c in range(D_A // LANES):
        cols = slice(c * LANES, (c + 1) * LANES)
        o_ref[:, cols] = (ot_scr[cols, :].T * _silu(ga_ref[:, cols])).astype(o_ref.dtype)


def _nsa_prompt(z, ck, cvt, sk, svt, wk, wvt):
    B, T, _ = z.shape
    n_cmp = ck.shape[1]
    et = _block_expand(LANES, T).T
    ovt = _overlap(n_cmp, LANES).T
    once = pl.Buffered(1)
    gq = GROUP_A * Q_BLOCK
    vt_w = N_KV_A * V_ROWS

    def whole(shape):
        return pl.BlockSpec((None,) + shape, lambda b, i: (b, 0, 0), pipeline_mode=once)

    return pl.pallas_call(
        _nsa_prompt_kernel,
        out_shape=jax.ShapeDtypeStruct((B, T, D_A), BF16),
        grid=(B, T // Q_BLOCK),
        in_specs=[pl.BlockSpec((None, Q_BLOCK, D_A), lambda b, i: (b, i, COL_Q // D_A)),
                  pl.BlockSpec((None, Q_BLOCK, D_A), lambda b, i: (b, i, COL_GA // D_A)),
                  pl.BlockSpec((None, Q_BLOCK, LANES), lambda b, i: (b, i, COL_G // LANES)),
                  whole((n_cmp, KV_W)), whole((vt_w, n_cmp)),
                  whole((T, KV_W)), whole((vt_w, T)), whole((T, KV_W)), whole((vt_w, T)),
                  pl.BlockSpec((T, LANES), lambda b, i: (0, 0), pipeline_mode=once),
                  pl.BlockSpec((LANES, n_cmp), lambda b, i: (0, 0), pipeline_mode=once)],
        out_specs=pl.BlockSpec((None, Q_BLOCK, D_A), lambda b, i: (b, i, 0)),
        scratch_shapes=[pltpu.VMEM((D_A, Q_BLOCK), BF16), pltpu.VMEM((D_A, Q_BLOCK), F32),
                        pltpu.VMEM((N_KV_A, 2 * LANES, gq), BF16),
                        pltpu.VMEM((N_KV_A, 1, gq), F32), pltpu.VMEM((N_KV_A, V_ROWS, gq), F32)],
        compiler_params=_cparams(("parallel", "arbitrary")),
        name="nsa_prompt",
    )(z, z, z, ck, cvt, sk, svt, wk, wvt, et, ovt)


def _nsa_sample_kernel(tab_ref, *refs, past, n_steps, t_new):
    pages = refs[:PAGES_PER_STEP]
    (q_ref, ckt_ref, cv_ref, new_ref, wkv_ref, g_ref, ga_ref, e_ref, ov_ref, o_ref, hist) = refs[PAGES_PER_STEP:]
    s = pl.program_id(1)
    for r in range(PAGES_PER_STEP):
        row = pl.multiple_of((s * PAGES_PER_STEP + r) * PAGE_SIZE, PAGE_SIZE)
        hist[pl.ds(row, PAGE_SIZE), :] = pages[r][...]

    @pl.when(s == 0)
    def _():
        hist[past:, :] = jnp.zeros((hist.shape[0] - past, KV_BRANCH), F32)
        hist[past:past + SUBLANES, :] = new_ref[...]

    @pl.when(s == n_steps - 1)
    def _():
        n_rows = N_HEADS_A * t_new
        n_keys = hist.shape[0]
        n_cmp = ckt_ref.shape[1]
        n_blk = ov_ref.shape[1]
        n_win = wkv_ref.shape[0]
        q = q_ref[...]
        row = lax.broadcasted_iota(jnp.int32, (n_rows, 1), 0)
        pos_q = past + row % t_new
        row_k = (row // t_new) % N_KV_A

        def diag(o_all):
            out = jnp.zeros((n_rows, HEAD_DIM_A), F32)
            for k in range(N_KV_A):
                out = out + jnp.where(row_k == k, o_all[:, k * HEAD_DIM_A:(k + 1) * HEAD_DIM_A], 0.0)
            return out

        cmp_end = lax.broadcasted_iota(jnp.int32, (1, n_cmp), 1) * CMP_STRIDE + (CMP_BLOCK - 1)
        s_c = jnp.dot(q, ckt_ref[...], preferred_element_type=F32)
        p = _masked_softmax(s_c, cmp_end <= pos_q).astype(BF16)
        o_cmp = diag(jnp.dot(p, cv_ref[...], preferred_element_type=F32))
        pj = jnp.dot(p, ov_ref[...], preferred_element_type=F32)
        rows_kq = N_KV_A * t_new
        p_slc = pj[0:rows_kq]
        for g in range(1, GROUP_A):
            p_slc = p_slc + pj[g * rows_kq:(g + 1) * rows_kq]

        j = lax.broadcasted_iota(jnp.int32, (rows_kq, n_blk), 1)
        pos_r = past + lax.broadcasted_iota(jnp.int32, (rows_kq, n_blk), 0) % t_new
        cur = pos_r // SEL_BLOCK
        valid = j * SEL_BLOCK <= pos_r
        forced = (j == 0) | (j == cur) | (j == cur - 1)
        score = jnp.where(valid, jnp.where(forced, FORCE_SCORE, p_slc), -jnp.inf)
        sel = _top_blocks(score, j, 1)
        sel = jnp.concatenate([sel] * GROUP_A, axis=0).astype(BF16)

        picked = jnp.dot(sel, e_ref[...], preferred_element_type=F32)
        hk = hist[:, 0:KV_W].astype(BF16)
        s_s = lax.dot_general(q, hk, (((1,), (1,)), ((), ())), preferred_element_type=F32)
        pos_k = lax.broadcasted_iota(jnp.int32, (1, n_keys), 1)
        p = _masked_softmax(s_s, (picked > 0.5) & (pos_k <= pos_q)).astype(BF16)
        o_sel = diag(jnp.dot(p, hist[:, KV_W:].astype(BF16), preferred_element_type=F32))

        wk = wkv_ref[:, 0:KV_W].astype(BF16)
        s_w = lax.dot_general(q, wk, (((1,), (1,)), ((), ())), preferred_element_type=F32)
        pos_w = past - WINDOW + lax.broadcasted_iota(jnp.int32, (1, n_win), 1)
        p = _masked_softmax(s_w, (pos_w <= pos_q) & (pos_q - pos_w < WINDOW)).astype(BF16)
        o_win = diag(jnp.dot(p, wkv_ref[:, KV_W:].astype(BF16), preferred_element_type=F32))

        gates = jax.nn.sigmoid(g_ref[...])
        o = gates[:, 0:1] * o_cmp + gates[:, 1:2] * o_sel + gates[:, 2:3] * o_win
        o_ref[...] = o * _silu(ga_ref[...])


def _nsa_sample(sel_view, table, qbs, ckt, cv, new_sel, wkv, g_rows, ga_rows, *, past, t_new):
    NB, n_pages = table.shape
    n_steps = n_pages // PAGES_PER_STEP
    n_keys = past + PAGE_SIZE
    n_cmp = ckt.shape[2]
    n_blk = 2 * LANES
    n_rows = qbs.shape[1]
    e = _block_expand(n_blk, n_keys)
    ov = _overlap(n_cmp, n_blk)

    def page_spec(r):
        return pl.BlockSpec((None, PAGE_SIZE, KV_BRANCH), lambda b, s, tab: (tab[b, s * PAGES_PER_STEP + r], 0, 0))

    def per_b(arr):
        return pl.BlockSpec((None,) + arr.shape[1:], lambda b, s, tab: (b, 0, 0))

    const2 = lambda b, s, tab: (0, 0)
    return pl.pallas_call(
        functools.partial(_nsa_sample_kernel, past=past, n_steps=n_steps, t_new=t_new),
        out_shape=jax.ShapeDtypeStruct((NB, n_rows, HEAD_DIM_A), F32),
        grid_spec=pltpu.PrefetchScalarGridSpec(
            num_scalar_prefetch=1,
            grid=(NB, n_steps),
            in_specs=[page_spec(r) for r in range(PAGES_PER_STEP)] + [
                per_b(qbs), per_b(ckt), per_b(cv), per_b(new_sel), per_b(wkv), per_b(g_rows), per_b(ga_rows),
                pl.BlockSpec(e.shape, const2, pipeline_mode=pl.Buffered(1)),
                pl.BlockSpec(ov.shape, const2, pipeline_mode=pl.Buffered(1))],
            out_specs=pl.BlockSpec((None, n_rows, HEAD_DIM_A), lambda b, s, tab: (b, 0, 0)),
            scratch_shapes=[pltpu.VMEM((n_keys, KV_BRANCH), F32)]),
        compiler_params=_cparams(("parallel", "arbitrary")),
        name="nsa_sample",
    )(table, *([sel_view] * PAGES_PER_STEP), qbs, ckt, cv, new_sel, wkv, g_rows, ga_rows, e, ov)


def _permute_w_in_a(w):
    c_kv, c_g, c_ga, c_glu, c_gb = 1024, 2560, 2608, 3632, 5680
    cols = [w[:, :c_kv], w[:, c_ga:c_glu], w[:, c_glu:c_gb], w[:, c_gb:], w[:, c_kv:c_g], w[:, c_g:c_ga]]
    n = sum(c.shape[1] for c in cols)
    return jnp.concatenate(cols + [jnp.zeros((w.shape[0], EVEN_N - n), w.dtype)], axis=1).astype(BF16)


def _pad_rows(x, front, total):
    return jnp.pad(x, ((0, 0), (front, total - front - x.shape[1]), (0, 0)))


def _even_prompt(z, cmpw, conv_args, *, tq):
    B, T, _ = z.shape
    kv = z[:, :, COL_KV:COL_G]
    kv_cmp, kv_sel, kv_win = (kv[:, :, br * KV_BRANCH:(br + 1) * KV_BRANCH] for br in range(3))
    n_pages = T // PAGE_SIZE
    table = jnp.arange(B * n_pages, dtype=jnp.int32).reshape(B, n_pages)
    ckv = _compress(kv_cmp.reshape(B * n_pages, SUBLANES, CMP_STRIDE * KV_BRANCH), table, *cmpw)

    def split_kv(x):
        return x[:, :, :KV_W].astype(BF16), _values_t(x[:, :, KV_W:])

    o_a = _nsa_prompt(z, *split_kv(ckv), *split_kv(kv_sel), *split_kv(kv_win))
    u = _glu(z, tt=tq)
    halo_map = lambda bb, i: (bb, jnp.maximum(i * (tq // CONV_HALO) - 1, 0), 0)
    o_b = _conv(u, halo_map, True, u, z, COL_GB // D_MODEL, *conv_args, tt=tq)
    kv_shape = (B, T, 2, N_KV_A, HEAD_DIM_A)
    w_keep = min(WINDOW, T)
    outs = (kv_cmp.reshape(kv_shape), kv_sel.reshape(kv_shape),
            kv_win[:, T - w_keep:].reshape(B, w_keep, 2, N_KV_A, HEAD_DIM_A), u[:, T - (CONV_WIDTH - 1):])
    return o_a, o_b, outs


def _even_sample(z, cmpw, conv_args, cmp_view, sel_view, table, win_buf, conv_hist, *, past):
    NB, t_new, _ = z.shape
    kv = z[:, :, COL_KV:COL_G]
    kv_cmp, kv_sel, kv_win = (kv[:, :, br * KV_BRANCH:(br + 1) * KV_BRANCH] for br in range(3))
    ckv = _compress(cmp_view, table, *cmpw)
    q = z[:, :, COL_Q:COL_Q + D_A].reshape(NB, t_new, N_KV_A, GROUP_A, HEAD_DIM_A) * (HEAD_DIM_A ** -0.5)
    qbs = jnp.einsum('bqkgd,kK->bgkqKd', q, jnp.eye(N_KV_A, dtype=F32)).reshape(NB, N_HEADS_A * t_new, KV_W)
    g_rows = z[:, :, COL_G:COL_G + 3 * N_HEADS_A].reshape(NB, t_new, 3, N_KV_A, GROUP_A)
    g_rows = jnp.transpose(g_rows, (0, 4, 3, 1, 2)).reshape(NB, N_HEADS_A * t_new, 3)
    g_rows = jnp.pad(g_rows, ((0, 0), (0, 0), (0, LANES - 3)))
    ga_rows = z[:, :, COL_GA:COL_GA + D_A].reshape(NB, t_new, N_KV_A, GROUP_A, HEAD_DIM_A)
    ga_rows = jnp.transpose(ga_rows, (0, 3, 2, 1, 4)).reshape(NB, N_HEADS_A * t_new, HEAD_DIM_A)
    wkv = jnp.concatenate([win_buf.reshape(NB, -1, KV_BRANCH), kv_win], axis=1)
    wb = win_buf.shape[1]
    o = _nsa_sample(sel_view, table, qbs.astype(BF16),
                    jnp.swapaxes(ckv[:, :, :KV_W], 1, 2).astype(BF16), ckv[:, :, KV_W:].astype(BF16),
                    _pad_rows(kv_sel, 0, SUBLANES), _pad_rows(wkv, 0, WIN_KEYS), g_rows, ga_rows,
                    past=past, t_new=t_new)
    o_a = jnp.transpose(o.reshape(NB, GROUP_A, N_KV_A, t_new, HEAD_DIM_A), (0, 3, 2, 1, 4))
    o_a = o_a.reshape(NB * t_new, D_A).astype(BF16)
    u = _glu(z.reshape(1, NB * t_new, EVEN_N), tt=NB * t_new).reshape(NB, t_new, D_MODEL)
    halo = _pad_rows(conv_hist, CONV_HALO - (CONV_WIDTH - 1), CONV_HALO)
    gate_b = _pad_rows(z[:, :, COL_GB:COL_GB + D_MODEL], 0, SUBLANES)
    o_b = _conv(halo, lambda bb, i: (bb, 0, 0), False, _pad_rows(u, 0, SUBLANES), gate_b, 0, *conv_args, tt=SUBLANES)
    o_b = o_b[:, :t_new].reshape(NB * t_new, D_MODEL)
    kv_shape = (NB, t_new, 2, N_KV_A, HEAD_DIM_A)
    outs = (kv_cmp.reshape(kv_shape), kv_sel.reshape(kv_shape),
            wkv[:, t_new:].reshape(NB, wb, 2, N_KV_A, HEAD_DIM_A),
            jnp.concatenate([conv_hist, u], axis=1)[:, -(CONV_WIDTH - 1):])
    return o_a, o_b, outs


def kernel(x_prompt, x_sample, mem_prompt, cache_cmp_kv, cache_sel_kv, cache_win_kv, state_conv, state_pool,
           cache_mem_kv, page_table, norm_mix, norm_mem, norm_out, w_in_a, w_out_a, cmp_pe, cmp_w1, cmp_w2,
           conv_w, conv_b, conv_ln_g, conv_ln_b, w_in_c, pool_w, pool_scale, w_out_c, w_mem_q, w_mem_kv, w_mem_o):
    B, T, D = x_prompt.shape
    NB, t_new, _ = x_sample.shape
    depth = norm_mix.shape[0]
    n_pool = cache_cmp_kv.shape[1]
    past = page_table.shape[1] * PAGE_SIZE
    mem_len = mem_prompt.shape[1]
    tm_p = 1024
    tq = 256
    ms = NB * t_new
    xp = x_prompt.reshape(B * T, D)
    xs = x_sample.reshape(ms, D)
    p_cmp, p_sel, p_win, p_conv, p_pool, p_mem = [], [], [], [], [], []
    s_cmp, s_sel, s_win, s_conv, s_pool = [], [], [], [], []
    cmp_view = cache_cmp_kv.reshape(-1, SUBLANES, CMP_STRIDE * KV_BRANCH)
    sel_view = cache_sel_kv.reshape(-1, PAGE_SIZE, KV_BRANCH)

    for layer in range(depth):
        if layer % 2 == 0:
            a = layer // 2
            w_in = _permute_w_in_a(w_in_a[a])
            w_out = w_out_a[a].astype(BF16)
            cmpw = _compress_weights(cmp_pe[a], cmp_w1[a], cmp_w2[a])
            conv_args = (conv_w[a], conv_b[a], conv_ln_g[a], conv_ln_b[a])
            zp = _mm([xp], w_in, gain=norm_mix[layer], tm=tm_p, tn=EVEN_TN).reshape(B, T, EVEN_N)
            o_a, o_b, outs = _even_prompt(zp, cmpw, conv_args, tq=tq)
            xp = _mm([o_a.reshape(B * T, D_A), o_b.reshape(B * T, D)], w_out, res=xp, tm=tm_p, tn=D)
            for lst, val in zip((p_cmp, p_sel, p_win, p_conv), outs):
                lst.append(val)
            zs = _mm([xs], w_in, gain=norm_mix[layer], tm=ms, tn=EVEN_TN).reshape(NB, t_new, EVEN_N)
            table = page_table + a * n_pool
            o_a, o_b, outs = _even_sample(zs, cmpw, conv_args, cmp_view, sel_view, table, cache_win_kv[a],
                                          state_conv[a], past=past)
            xs = _mm([o_a, o_b], w_out, res=xs, tm=ms, tn=D)
            for lst, val in zip((s_cmp, s_sel, s_win, s_conv), outs):
                lst.append(val)
        else:
            c = layer // 2
            w_in = w_in_c[c].astype(BF16)
            w_out = w_out_c[c].astype(BF16)
            zp = _mm([xp], w_in, gain=norm_mix[layer], tm=tm_p, tn=D).reshape(B, T, 2 * D)
            halo_map = lambda bb, i: (bb, jnp.maximum(i * (tq // POOL_HALO) - 1, 0), 0)
            m = _pool(zp, halo_map, True, zp, pool_w[c], pool_scale[c], tt=tq, pos0=0)
            xp = _mm([m.reshape(B * T, D)], w_out, res=xp, tm=tm_p, tn=D)
            p_pool.append(zp[:, T - POOL_HIST:, :D])
            zs = _mm([xs], w_in, gain=norm_mix[layer], tm=ms, tn=D).reshape(NB, t_new, 2 * D)
            hist = state_pool[c]
            m = _pool(_pad_rows(hist, POOL_HALO - POOL_HIST, POOL_HALO), lambda bb, i: (bb, 0, 0), False,
                      _pad_rows(zs, 0, SUBLANES), pool_w[c], pool_scale[c], tt=SUBLANES, pos0=past)
            xs = _mm([m[:, :t_new].reshape(ms, D)], w_out, res=xs, tm=ms, tn=D)
            s_pool.append(jnp.concatenate([hist, zs[:, :, :D]], axis=1)[:, -POOL_HIST:])

        wq = w_mem_q[layer].astype(BF16)
        wo = w_mem_o[layer].astype(BF16)
        mkv = _mm([mem_prompt.reshape(B * mem_len, D)], w_mem_kv[layer].astype(BF16), tm=B * mem_len, tn=D)
        mkv = mkv.reshape(B, mem_len, 2 * D)
        p_mem.append(mkv.reshape(B, mem_len, 2, N_HEADS_M, HEAD_DIM_M))
        q = _mm([xp], wq, gain=norm_mem[layer], tm=tm_p, tn=D, out_dtype=BF16).reshape(B, T, D)
        o = _mem_attn(q, jnp.swapaxes(mkv[:, :, :D], 1, 2).astype(BF16), mkv[:, :, D:].astype(BF16), tm=tq)
        xp = _mm([o.reshape(B * T, D)], wo, res=xp, tm=tm_p, tn=D)
        ckv = cache_mem_kv[layer].reshape(NB, mem_len, 2 * D)
        q = _mm([xs], wq, gain=norm_mem[layer], tm=ms, tn=D, out_dtype=BF16).reshape(NB, t_new, D)
        o = _mem_attn(_pad_rows(q, 0, SUBLANES), jnp.swapaxes(ckv[:, :, :D], 1, 2).astype(BF16),
                      ckv[:, :, D:].astype(BF16), tm=SUBLANES)
        xs = _mm([o[:, :t_new].reshape(ms, D)], wo, res=xs, tm=ms, tn=D)

    y_prompt = _rmsnorm(xp, norm_out, tm=tm_p).reshape(B, T, D)
    y_sample = _rmsnorm(xs, norm_out, tm=ms).reshape(NB, t_new, D)
    return (y_prompt, y_sample,
            jnp.stack(p_cmp), jnp.stack(p_sel), jnp.stack(p_win), jnp.stack(p_conv), jnp.stack(p_pool),
            jnp.stack(p_mem),
            jnp.stack(s_cmp), jnp.stack(s_sel), jnp.stack(s_win), jnp.stack(s_conv), jnp.stack(s_pool))
```

```python
import functools

import jax
import jax.numpy as jnp
from jax import lax
from jax.experimental import pallas as pl
from jax.experimental.pallas import tpu as pltpu

F32 = jnp.float32
BF16 = jnp.bfloat16

LANES = 128
SUBLANES = 8
VMEM_LIMIT_BYTES = 56 * 1024 * 1024

D_MODEL = 1024
N_HEADS_A = 16
HEAD_DIM_A = 64
N_KV_A = 4
GROUP_A = N_HEADS_A // N_KV_A
D_A = N_HEADS_A * HEAD_DIM_A
KV_W = N_KV_A * HEAD_DIM_A
KV_BRANCH = 2 * KV_W
CMP_STRIDE = 16
CMP_BLOCK = 32
CMP_HIDDEN = 128
SEL_BLOCK = 64
N_SEL = 16
WINDOW = 512
Q_BLOCK = 128
FORCE_SCORE = 1000.0
CONV_WIDTH = 31
CONV_HALO = 32
POOL_WINDOWS = (2, 4, 8, 16)
POOL_GROUP = D_MODEL // len(POOL_WINDOWS)
POOL_HIST = 15
POOL_HALO = 16
PAGE_SIZE = 128
N_HEADS_M = 4
HEAD_DIM_M = 256
NORM_EPS = 1e-6
NEG = -1e30

COL_Q, COL_GA, COL_VAL, COL_GL, COL_GB = 0, 1024, 2048, 3072, 4096
COL_KV = 5120
COL_G = COL_KV + 3 * KV_BRANCH
EVEN_N = 6912
EVEN_TN = 768
PAGES_PER_STEP = 8


def _cparams(sem):
    return pltpu.CompilerParams(dimension_semantics=sem, vmem_limit_bytes=VMEM_LIMIT_BYTES)


def _silu(x):
    return x * jax.nn.sigmoid(x)


def _mm_kernel(*refs, n_parts, has_gain, has_res):
    parts = refs[:n_parts]
    pos = n_parts
    gain_ref = refs[pos] if has_gain else None
    pos += int(has_gain)
    w_ref = refs[pos]
    pos += 1
    res_ref = refs[pos] if has_res else None
    pos += int(has_res)
    o_ref, lhs = refs[pos], refs[pos + 1]

    @pl.when(pl.program_id(1) == 0)
    def _():
        off = 0
        for p in parts:
            x = p[...].astype(F32)
            if has_gain:
                x = x * lax.rsqrt(jnp.mean(x * x, axis=-1, keepdims=True) + NORM_EPS) * gain_ref[...]
            lhs[:, off:off + p.shape[1]] = x.astype(BF16)
            off += p.shape[1]

    acc = jnp.dot(lhs[...], w_ref[...], preferred_element_type=F32)
    if has_res:
        acc = acc + res_ref[...]
    o_ref[...] = acc.astype(o_ref.dtype)


def _mm(parts, w, *, gain=None, res=None, tm, tn, out_dtype=F32):
    M = parts[0].shape[0]
    K, N = w.shape
    assert sum(p.shape[1] for p in parts) == K and M % tm == 0 and N % tn == 0
    in_specs = [pl.BlockSpec((tm, p.shape[1]), lambda i, j: (i, 0)) for p in parts]
    args = list(parts)
    if gain is not None:
        in_specs.append(pl.BlockSpec((1, K), lambda i, j: (0, 0)))
        args.append(gain.reshape(1, K).astype(F32))
    in_specs.append(pl.BlockSpec((K, tn), lambda i, j: (0, j)))
    args.append(w)
    if res is not None:
        in_specs.append(pl.BlockSpec((tm, tn), lambda i, j: (i, j)))
        args.append(res)
    return pl.pallas_call(
        functools.partial(_mm_kernel, n_parts=len(parts), has_gain=gain is not None, has_res=res is not None),
        out_shape=jax.ShapeDtypeStruct((M, N), out_dtype),
        grid=(M // tm, N // tn),
        in_specs=in_specs,
        out_specs=pl.BlockSpec((tm, tn), lambda i, j: (i, j)),
        scratch_shapes=[pltpu.VMEM((tm, K), BF16)],
        compiler_params=_cparams(("parallel", "arbitrary")),
        name="mm",
    )(*args)


def _rmsnorm_kernel(x_ref, g_ref, o_ref):
    x = x_ref[...]
    o_ref[...] = x * lax.rsqrt(jnp.mean(x * x, axis=-1, keepdims=True) + NORM_EPS) * g_ref[...]


def _rmsnorm(x, g, *, tm):
    M, D = x.shape
    return pl.pallas_call(
        _rmsnorm_kernel,
        out_shape=jax.ShapeDtypeStruct((M, D), F32),
        grid=(M // tm,),
        in_specs=[pl.BlockSpec((tm, D), lambda i: (i, 0)), pl.BlockSpec((1, D), lambda i: (0, 0))],
        out_specs=pl.BlockSpec((tm, D), lambda i: (i, 0)),
        compiler_params=_cparams(("parallel",)),
        name="rmsnorm",
    )(x, g.reshape(1, D))


def _glu_kernel(val_ref, gl_ref, o_ref):
    o_ref[...] = val_ref[...] * jax.nn.sigmoid(gl_ref[...])


def _glu(z, *, tt):
    B, T, _ = z.shape
    blk = (None, tt, D_MODEL)
    return pl.pallas_call(
        _glu_kernel,
        out_shape=jax.ShapeDtypeStruct((B, T, D_MODEL), F32),
        grid=(B, T // tt),
        in_specs=[pl.BlockSpec(blk, lambda b, i: (b, i, COL_VAL // D_MODEL)),
                  pl.BlockSpec(blk, lambda b, i: (b, i, COL_GL // D_MODEL))],
        out_specs=pl.BlockSpec(blk, lambda b, i: (b, i, 0)),
        compiler_params=_cparams(("parallel", "parallel")),
        name="glu",
    )(z, z)


CONV_ROWS = 32


def _conv_kernel(halo_ref, cur_ref, gate_ref, w_ref, b_ref, lg_ref, lb_ref, o_ref, buf, y_scr, *, tt, zero_first_halo):
    halo = halo_ref[...]
    if zero_first_halo:
        halo = jnp.where(pl.program_id(1) == 0, 0.0, halo)
    buf[0:CONV_HALO, :] = halo
    buf[CONV_HALO:CONV_HALO + tt, :] = cur_ref[...]
    first = CONV_HALO - (CONV_WIDTH - 1)
    rc = min(CONV_ROWS, tt)
    for r0 in range(0, tt, rc):
        acc = jnp.broadcast_to(b_ref[...], (rc, D_MODEL))
        for j in range(CONV_WIDTH):
            acc = acc + w_ref[j:j + 1, :] * buf[first + r0 + j:first + r0 + j + rc, :]
        y_scr[r0:r0 + rc, :] = acc
    y = y_scr[...]
    xc = y - jnp.mean(y, axis=-1, keepdims=True)
    yn = xc * lax.rsqrt(jnp.mean(xc * xc, axis=-1, keepdims=True) + NORM_EPS) * lg_ref[...] + lb_ref[...]
    o_ref[...] = (_silu(yn) * _silu(gate_ref[...])).astype(o_ref.dtype)


def _conv(halo_arr, halo_map, zero_first_halo, u, gate_arr, gate_col, w, b, lg, lb, *, tt):
    B, T, _ = u.shape
    w32 = jnp.concatenate([w, jnp.zeros((CONV_HALO - CONV_WIDTH + 1, D_MODEL), F32)], axis=0)[:CONV_HALO]
    vec = pl.BlockSpec((1, D_MODEL), lambda bb, i: (0, 0))
    return pl.pallas_call(
        functools.partial(_conv_kernel, tt=tt, zero_first_halo=zero_first_halo),
        out_shape=jax.ShapeDtypeStruct((B, T, D_MODEL), BF16),
        grid=(B, T // tt),
        in_specs=[pl.BlockSpec((None, CONV_HALO, D_MODEL), halo_map),
                  pl.BlockSpec((None, tt, D_MODEL), lambda bb, i: (bb, i, 0)),
                  pl.BlockSpec((None, tt, D_MODEL), lambda bb, i: (bb, i, gate_col)),
                  pl.BlockSpec((CONV_HALO, D_MODEL), lambda bb, i: (0, 0)),
                  vec, vec, vec],
        out_specs=pl.BlockSpec((None, tt, D_MODEL), lambda bb, i: (bb, i, 0)),
        scratch_shapes=[pltpu.VMEM((CONV_HALO + tt, D_MODEL), F32), pltpu.VMEM((tt, D_MODEL), F32)],
        compiler_params=_cparams(("parallel", "parallel")),
        name="conv",
    )(halo_arr, u, gate_arr, w32, b.reshape(1, -1), lg.reshape(1, -1), lb.reshape(1, -1))


def _pool_kernel(halo_ref, cur_ref, gate_ref, pw_ref, ps_ref, o_ref, buf, *, tt, zero_first_halo, pos0):
    i = pl.program_id(1)
    halo = halo_ref[...]
    if zero_first_halo:
        halo = jnp.where(i == 0, 0.0, halo)
    buf[0:POOL_HALO, :] = halo
    buf[POOL_HALO:POOL_HALO + tt, :] = cur_ref[...]
    pos = pos0 + i * tt + lax.broadcasted_iota(jnp.int32, (tt, 1), 0)
    for g, w in enumerate(POOL_WINDOWS):
        c0, c1 = g * POOL_GROUP, (g + 1) * POOL_GROUP
        s = buf[POOL_HALO:POOL_HALO + tt, c0:c1]
        for j in range(1, w):
            s = s + buf[POOL_HALO - j:POOL_HALO - j + tt, c0:c1]
        cnt = jnp.minimum(w, pos + 1).astype(F32)
        d = s / cnt - buf[POOL_HALO:POOL_HALO + tt, c0:c1]
        m = jnp.dot(d.astype(BF16), pw_ref[g], preferred_element_type=F32)
        o_ref[:, c0:c1] = (m * ps_ref[:, c0:c1] * _silu(gate_ref[:, c0:c1])).astype(o_ref.dtype)


def _pool(halo_arr, halo_map, zero_first_halo, z, pool_w, pool_scale, *, tt, pos0):
    B, T, _ = z.shape
    return pl.pallas_call(
        functools.partial(_pool_kernel, tt=tt, zero_first_halo=zero_first_halo, pos0=pos0),
        out_shape=jax.ShapeDtypeStruct((B, T, D_MODEL), BF16),
        grid=(B, T // tt),
        in_specs=[pl.BlockSpec((None, POOL_HALO, D_MODEL), halo_map),
                  pl.BlockSpec((None, tt, D_MODEL), lambda bb, i: (bb, i, 0)),
                  pl.BlockSpec((None, tt, D_MODEL), lambda bb, i: (bb, i, 1)),
                  pl.BlockSpec((len(POOL_WINDOWS), POOL_GROUP, POOL_GROUP), lambda bb, i: (0, 0, 0)),
                  pl.BlockSpec((1, D_MODEL), lambda bb, i: (0, 0))],
        out_specs=pl.BlockSpec((None, tt, D_MODEL), lambda bb, i: (bb, i, 0)),
        scratch_shapes=[pltpu.VMEM((POOL_HALO + tt, D_MODEL), F32)],
        compiler_params=_cparams(("parallel", "parallel")),
        name="pool",
    )(halo_arr, z, z, pool_w.astype(BF16), pool_scale.reshape(1, -1))


def _mem_attn_kernel(q_ref, kt_ref, v_ref, o_ref):
    for h in range(N_HEADS_M):
        c0, c1 = h * HEAD_DIM_M, (h + 1) * HEAD_DIM_M
        q = q_ref[:, c0:c1] * (HEAD_DIM_M ** -0.5)
        s = jnp.dot(q, kt_ref[c0:c1, :], preferred_element_type=F32)
        e = jnp.exp(s - jnp.max(s, axis=-1, keepdims=True))
        p = e / jnp.sum(e, axis=-1, keepdims=True)
        o = jnp.dot(p.astype(BF16), v_ref[:, c0:c1], preferred_element_type=F32)
        o_ref[:, c0:c1] = o.astype(o_ref.dtype)


def _mem_attn(q, kt, v, *, tm):
    NB, Tq, D = q.shape
    M = v.shape[1]
    return pl.pallas_call(
        _mem_attn_kernel,
        out_shape=jax.ShapeDtypeStruct((NB, Tq, D), BF16),
        grid=(NB, Tq // tm),
        in_specs=[pl.BlockSpec((None, tm, D), lambda b, i: (b, i, 0)),
                  pl.BlockSpec((None, D, M), lambda b, i: (b, 0, 0)),
                  pl.BlockSpec((None, M, D), lambda b, i: (b, 0, 0))],
        out_specs=pl.BlockSpec((None, tm, D), lambda b, i: (b, i, 0)),
        compiler_params=_cparams(("parallel", "parallel")),
        name="mem_attn",
    )(q, kt, v)


def _compress_weights(pe, w1, w2):
    eye2 = jnp.eye(2, dtype=F32)
    w1bd = jnp.stack([jnp.einsum('ldh,kK->lkdKh', w1[c], eye2).reshape(CMP_BLOCK * LANES, 2 * CMP_HIDDEN)
                      for c in range(2)]).astype(BF16)
    w2bd = jnp.stack([jnp.einsum('hd,kK->khKd', w2[c], eye2).reshape(2 * CMP_HIDDEN, LANES)
                      for c in range(2)]).astype(BF16)
    pe_rows = jnp.zeros((2, 2 * SUBLANES, CMP_BLOCK * HEAD_DIM_A), F32).at[:, 0].set(pe.reshape(2, -1))
    bias = jnp.stack([_mm([pe_rows[c]], w1[c].reshape(CMP_BLOCK * HEAD_DIM_A, CMP_HIDDEN).astype(BF16),
                          tm=2 * SUBLANES, tn=CMP_HIDDEN)[0] for c in range(2)])
    bias2 = jnp.concatenate([bias, bias], axis=1).reshape(2, 1, 2 * CMP_HIDDEN)
    return w1bd, w2bd, bias2


def _transpose_page(page_t):
    eye = (lax.broadcasted_iota(jnp.int32, (PAGE_SIZE, PAGE_SIZE), 0)
           == lax.broadcasted_iota(jnp.int32, (PAGE_SIZE, PAGE_SIZE), 1)).astype(BF16)
    return lax.dot_general(eye, page_t.astype(BF16), (((1,), (1,)), ((), ())), preferred_element_type=F32)


def _compress_kernel(tab_ref, *refs, n_pos, n_steps, transposed):
    pages = refs[:PAGES_PER_STEP]
    w1_ref, bias_ref, w2_ref, o_ref, hist = refs[PAGES_PER_STEP:]
    s = pl.program_id(1)
    n_grp = KV_BRANCH // LANES
    for r in range(PAGES_PER_STEP):
        row = pl.multiple_of((s * PAGES_PER_STEP + r) * PAGE_SIZE, PAGE_SIZE)
        x = _transpose_page(pages[r][...]) if transposed else pages[r][...]
        for grp in range(n_grp):
            hist[grp, pl.ds(row, PAGE_SIZE), :] = x[:, grp * LANES:(grp + 1) * LANES]

    @pl.when(s == 0)
    def _():
        hist[:, n_pos:, :] = jnp.zeros((n_grp, hist.shape[1] - n_pos, LANES), F32)

    @pl.when(s == n_steps - 1)
    def _():
        n_rows = n_pos // CMP_STRIDE
        for grp in range(n_grp):
            c, kp = divmod(grp, 2)
            base = grp * LANES
            xg = jnp.concatenate([hist[grp, pl.ds(l, n_rows, stride=CMP_STRIDE), :]
                                  for l in range(CMP_BLOCK)], axis=1).astype(BF16)
            hid = _silu(jnp.dot(xg, w1_ref[c], preferred_element_type=F32) + bias_ref[c])
            o_ref[:, base:base + LANES] = jnp.dot(hid.astype(BF16), w2_ref[c], preferred_element_type=F32)


def _compress(pages, table, w1bd, w2bd, bias2, *, transposed):
    NB, n_pages = table.shape
    n_steps = n_pages // PAGES_PER_STEP
    n_pos = n_pages * PAGE_SIZE
    n_rows = n_pos // CMP_STRIDE

    def page_spec(r):
        return pl.BlockSpec((None,) + pages.shape[1:], lambda b, s, tab: (tab[b, s * PAGES_PER_STEP + r], 0, 0))

    const3 = lambda b, s, tab: (0, 0, 0)
    return pl.pallas_call(
        functools.partial(_compress_kernel, n_pos=n_pos, n_steps=n_steps, transposed=transposed),
        out_shape=jax.ShapeDtypeStruct((NB, n_rows, KV_BRANCH), F32),
        grid_spec=pltpu.PrefetchScalarGridSpec(
            num_scalar_prefetch=1,
            grid=(NB, n_steps),
            in_specs=[page_spec(r) for r in range(PAGES_PER_STEP)] + [
                pl.BlockSpec(w1bd.shape, const3), pl.BlockSpec(bias2.shape, const3), pl.BlockSpec(w2bd.shape, const3)],
            out_specs=pl.BlockSpec((None, n_rows, KV_BRANCH), lambda b, s, tab: (b, 0, 0)),
            scratch_shapes=[pltpu.VMEM((KV_BRANCH // LANES, n_pos + CMP_STRIDE, LANES), F32)]),
        compiler_params=_cparams(("parallel", "arbitrary")),
        name="compress",
    )(table, *([pages] * PAGES_PER_STEP), w1bd, bias2, w2bd)


def _overlap(n_cmp_rows, n_sel_cols):
    c0 = jnp.arange(n_cmp_rows) * CMP_STRIDE
    s0 = jnp.arange(n_sel_cols) * SEL_BLOCK
    ov = jnp.minimum(c0[:, None] + CMP_BLOCK, s0[None, :] + SEL_BLOCK) - jnp.maximum(c0[:, None], s0[None, :])
    return (jnp.clip(ov, 0, None).astype(F32) / CMP_BLOCK).astype(BF16)


def _block_expand(n_blocks, n_keys):
    return (jnp.arange(n_keys)[None, :] // SEL_BLOCK == jnp.arange(n_blocks)[:, None]).astype(BF16)


def _masked_softmax(s, mask, axis=-1):
    m = jnp.max(jnp.where(mask, s, NEG), axis=axis, keepdims=True)
    e = jnp.where(mask, jnp.exp(s - m), 0.0)
    d = jnp.sum(e, axis=axis, keepdims=True)
    return e / jnp.where(d > 0, d, 1.0)


def _top_blocks(score, idx, axis):
    n = score.shape[axis]
    sel = jnp.zeros(score.shape, F32)
    work = score
    for _ in range(N_SEL):
        m = jnp.max(work, axis=axis, keepdims=True)
        first = jnp.min(jnp.where(work == m, idx, n), axis=axis, keepdims=True)
        pick = idx == first
        sel = jnp.where(pick & (m > -jnp.inf), 1.0, sel)
        work = jnp.where(pick, -jnp.inf, work)
    return sel


SEL_TK = 512
WIN_KEYS = WINDOW + Q_BLOCK
LOG2E = 1.4426950408889634
V_ROWS = HEAD_DIM_A + 16


def _values_t(v):
    B, n, _ = v.shape
    vt = jnp.swapaxes(v, 1, 2).reshape(B, N_KV_A, HEAD_DIM_A, n)
    pad = jnp.zeros((B, N_KV_A, V_ROWS - HEAD_DIM_A, n), F32).at[:, :, 0].set(1.0)
    return jnp.concatenate([vt, pad], axis=2).reshape(B, N_KV_A * V_ROWS, n).astype(BF16)


def _nsa_prompt_kernel(q_ref, ga_ref, g_ref, ck_ref, cvt_ref, sk_ref, svt_ref, wk_ref, wvt_ref, et_ref, ovt_ref,
                       o_ref, qt_scr, ot_scr, qb_scr, m_scr, acc_scr):
    i = pl.program_id(1)
    q0 = i * Q_BLOCK
    n_cmp = ck_ref.shape[0]
    pos_q = q0 + lax.broadcasted_iota(jnp.int32, (1, Q_BLOCK), 1)
    for c in range(D_A // LANES):
        cols = slice(c * LANES, (c + 1) * LANES)
        qt_scr[cols, :] = (q_ref[:, cols].T * (HEAD_DIM_A ** -0.5 * LOG2E)).astype(BF16)
    gates_t = jax.nn.sigmoid(g_ref[...]).T
    j_t = lax.broadcasted_iota(jnp.int32, (LANES, Q_BLOCK), 0)
    cur_t = pos_q // SEL_BLOCK
    valid_t = j_t * SEL_BLOCK <= pos_q
    forced_t = (j_t == 0) | (j_t == cur_t) | (j_t == cur_t - 1)
    cmp_end = lax.broadcasted_iota(jnp.int32, (n_cmp, 1), 0) * CMP_STRIDE + (CMP_BLOCK - 1)
    cbias = jnp.where(cmp_end <= pos_q, 0.0, NEG)
    cvalid = jnp.where(pos_q >= CMP_BLOCK - 1, 1.0, 0.0)
    w0 = pl.multiple_of(jnp.maximum(q0 - WINDOW, 0), Q_BLOCK)
    pos_w = w0 + lax.broadcasted_iota(jnp.int32, (WIN_KEYS, 1), 0)
    wbias = jnp.where((pos_w <= pos_q) & (pos_q - pos_w < WINDOW), 0.0, NEG)
    n_full = q0 // SEL_TK
    zero_half = jnp.zeros((HEAD_DIM_A, Q_BLOCK), BF16)

    def weights_t(s, bias):
        out = []
        for g in range(GROUP_A):
            sg = s[:, g * Q_BLOCK:(g + 1) * Q_BLOCK] + bias
            out.append(jnp.exp2(sg - jnp.max(sg, axis=0, keepdims=True)).astype(BF16))
        return jnp.concatenate(out, axis=1)

    def normalise(o_ext, scale=None):
        d = o_ext[HEAD_DIM_A:HEAD_DIM_A + 1, :]
        inv = 1.0 / jnp.where(d > 0, d, 1.0)
        return inv if scale is None else inv * scale

    for k in range(N_KV_A):
        vr = slice(k * V_ROWS, (k + 1) * V_ROWS)
        pr = slice((k // 2) * LANES, (k // 2 + 1) * LANES)
        blocks = []
        for g in range(GROUP_A):
            h = k * GROUP_A + g
            qh = qt_scr[h * HEAD_DIM_A:(h + 1) * HEAD_DIM_A, :]
            blocks.append(jnp.concatenate([qh, zero_half] if k % 2 == 0 else [zero_half, qh], axis=0))
        qt = jnp.concatenate(blocks, axis=1)

        s = jnp.dot(ck_ref[:, pr], qt, preferred_element_type=F32)
        e = weights_t(s, cbias)
        o_ext = jnp.dot(cvt_ref[vr, :], e, preferred_element_type=F32)
        inv = normalise(o_ext, jnp.concatenate([cvalid] * GROUP_A, axis=1))
        o_cmp = o_ext[:HEAD_DIM_A] * inv
        pj = jnp.dot(ovt_ref[...], e, preferred_element_type=F32) * inv
        p_slc = pj[:, 0:Q_BLOCK]
        for g in range(1, GROUP_A):
            p_slc = p_slc + pj[:, g * Q_BLOCK:(g + 1) * Q_BLOCK]

        score_t = jnp.where(valid_t, jnp.where(forced_t, FORCE_SCORE, p_slc), -jnp.inf)
        sel_bias = ((_top_blocks(score_t, j_t, 0) - 1.0) * (-NEG)).astype(BF16)
        qb = jnp.concatenate([qt, jnp.concatenate([sel_bias] * GROUP_A, axis=1)], axis=0)

        qb_scr[k] = qb

        s = jnp.dot(wk_ref[pl.ds(w0, WIN_KEYS), pr], qt, preferred_element_type=F32)
        o_ext = jnp.dot(wvt_ref[vr, pl.ds(w0, WIN_KEYS)], weights_t(s, wbias), preferred_element_type=F32)
        o_win = o_ext[:HEAD_DIM_A] * normalise(o_ext)

        for g in range(GROUP_A):
            h = k * GROUP_A + g
            lanes = slice(g * Q_BLOCK, (g + 1) * Q_BLOCK)
            ot_scr[h * HEAD_DIM_A:(h + 1) * HEAD_DIM_A, :] = (
                gates_t[h:h + 1, :] * o_cmp[:, lanes]
                + gates_t[2 * N_HEADS_A + h:2 * N_HEADS_A + h + 1, :] * o_win[:, lanes])

    m_scr[...] = jnp.full(m_scr.shape, NEG, F32)
    acc_scr[...] = jnp.zeros(acc_scr.shape, F32)

    def sel_step(t, causal):
        k0 = pl.multiple_of(t * SEL_TK, SEL_TK)
        onehot = et_ref[pl.ds(k0, SEL_TK), :]
        if causal:
            pos_k = k0 + lax.broadcasted_iota(jnp.int32, (SEL_TK, 1), 0)
            future = jnp.where(pos_k <= pos_q, 0.0, NEG)

        def scores(k):
            pr = slice((k // 2) * LANES, (k // 2 + 1) * LANES)
            kb = jnp.concatenate([sk_ref[pl.ds(k0, SEL_TK), pr], onehot], axis=1)
            return jnp.dot(kb, qb_scr[k], preferred_element_type=F32)

        s_next = scores(0)
        for k in range(N_KV_A):
            s = s_next
            if k + 1 < N_KV_A:
                s_next = scores(k + 1)
            es, alphas = [], []
            for g in range(GROUP_A):
                lanes = slice(g * Q_BLOCK, (g + 1) * Q_BLOCK)
                sg = s[:, lanes] + future if causal else s[:, lanes]
                m_old = m_scr[k, :, lanes]
                m_new = jnp.maximum(m_old, jnp.max(sg, axis=0, keepdims=True))
                alphas.append(jnp.exp2(m_old - m_new))
                es.append(jnp.exp2(sg - m_new).astype(BF16))
                m_scr[k, :, lanes] = m_new
            pv = jnp.dot(svt_ref[k * V_ROWS:(k + 1) * V_ROWS, pl.ds(k0, SEL_TK)], jnp.concatenate(es, axis=1),
                         preferred_element_type=F32)
            acc_scr[k] = jnp.concatenate(alphas, axis=1) * acc_scr[k] + pv

    def full_step(t, carry):
        sel_step(t, False)
        return carry

    lax.fori_loop(0, n_full, full_step, 0)
    sel_step(n_full, True)
    for k in range(N_KV_A):
        acc = acc_scr[k]
        o_sel = acc[:HEAD_DIM_A] * normalise(acc)
        for g in range(GROUP_A):
            h = k * GROUP_A + g
            lanes = slice(g * Q_BLOCK, (g + 1) * Q_BLOCK)
            rows = slice(h * HEAD_DIM_A, (h + 1) * HEAD_DIM_A)
            ot_scr[rows, :] = ot_scr[rows, :] + gates_t[N_HEADS_A + h:N_HEADS_A + h + 1, :] * o_sel[:, lanes]

    for c in range(D_A // LANES):
        cols = slice(c * LANES, (c + 1) * LANES)
        o_ref[:, cols] = (ot_scr[cols, :].T * _silu(ga_ref[:, cols])).astype(o_ref.dtype)


def _nsa_prompt(z, ck, cvt, sk, svt, wk, wvt):
    B, T, _ = z.shape
    n_cmp = ck.shape[1]
    et = _block_expand(LANES, T).T
    ovt = _overlap(n_cmp, LANES).T
    once = pl.Buffered(1)
    gq = GROUP_A * Q_BLOCK
    vt_w = N_KV_A * V_ROWS

    def whole(shape):
        return pl.BlockSpec((None,) + shape, lambda b, i: (b, 0, 0), pipeline_mode=once)

    return pl.pallas_call(
        _nsa_prompt_kernel,
        out_shape=jax.ShapeDtypeStruct((B, T, D_A), BF16),
        grid=(B, T // Q_BLOCK),
        in_specs=[pl.BlockSpec((None, Q_BLOCK, D_A), lambda b, i: (b, i, COL_Q // D_A)),
                  pl.BlockSpec((None, Q_BLOCK, D_A), lambda b, i: (b, i, COL_GA // D_A)),
                  pl.BlockSpec((None, Q_BLOCK, LANES), lambda b, i: (b, i, COL_G // LANES)),
                  whole((n_cmp, KV_W)), whole((vt_w, n_cmp)),
                  whole((T, KV_W)), whole((vt_w, T)), whole((T, KV_W)), whole((vt_w, T)),
                  pl.BlockSpec((T, LANES), lambda b, i: (0, 0), pipeline_mode=once),
                  pl.BlockSpec((LANES, n_cmp), lambda b, i: (0, 0), pipeline_mode=once)],
        out_specs=pl.BlockSpec((None, Q_BLOCK, D_A), lambda b, i: (b, i, 0)),
        scratch_shapes=[pltpu.VMEM((D_A, Q_BLOCK), BF16), pltpu.VMEM((D_A, Q_BLOCK), F32),
                        pltpu.VMEM((N_KV_A, 2 * LANES, gq), BF16),
                        pltpu.VMEM((N_KV_A, 1, gq), F32), pltpu.VMEM((N_KV_A, V_ROWS, gq), F32)],
        compiler_params=_cparams(("parallel", "arbitrary")),
        name="nsa_prompt",
    )(z, z, z, ck, cvt, sk, svt, wk, wvt, et, ovt)


def _nsa_sample_kernel(tab_ref, *refs, past, n_steps, t_new):
    pages = refs[:PAGES_PER_STEP]
    (q_ref, ckt_ref, cv_ref, new_ref, wkv_ref, g_ref, ga_ref, e_ref, ov_ref, o_ref, hist) = refs[PAGES_PER_STEP:]
    s = pl.program_id(1)
    for r in range(PAGES_PER_STEP):
        col = pl.multiple_of((s * PAGES_PER_STEP + r) * PAGE_SIZE, PAGE_SIZE)
        hist[:, pl.ds(col, PAGE_SIZE)] = pages[r][...]

    @pl.when(s == 0)
    def _():
        hist[:, past:] = new_ref[...]

    @pl.when(s == n_steps - 1)
    def _():
        n_rows = N_HEADS_A * t_new
        n_keys = hist.shape[1]
        n_cmp = ckt_ref.shape[1]
        n_blk = ov_ref.shape[1]
        n_win = wkv_ref.shape[0]
        q = q_ref[...]
        row = lax.broadcasted_iota(jnp.int32, (n_rows, 1), 0)
        pos_q = past + row % t_new
        row_k = (row // t_new) % N_KV_A

        def diag(o_all):
            out = jnp.zeros((n_rows, HEAD_DIM_A), F32)
            for k in range(N_KV_A):
                out = out + jnp.where(row_k == k, o_all[:, k * HEAD_DIM_A:(k + 1) * HEAD_DIM_A], 0.0)
            return out

        cmp_end = lax.broadcasted_iota(jnp.int32, (1, n_cmp), 1) * CMP_STRIDE + (CMP_BLOCK - 1)
        s_c = jnp.dot(q, ckt_ref[...], preferred_element_type=F32)
        p = _masked_softmax(s_c, cmp_end <= pos_q).astype(BF16)
        o_cmp = diag(jnp.dot(p, cv_ref[...], preferred_element_type=F32))
        pj = jnp.dot(p, ov_ref[...], preferred_element_type=F32)
        rows_kq = N_KV_A * t_new
        p_slc = pj[0:rows_kq]
        for g in range(1, GROUP_A):
            p_slc = p_slc + pj[g * rows_kq:(g + 1) * rows_kq]

        j = lax.broadcasted_iota(jnp.int32, (rows_kq, n_blk), 1)
        pos_r = past + lax.broadcasted_iota(jnp.int32, (rows_kq, n_blk), 0) % t_new
        cur = pos_r // SEL_BLOCK
        valid = j * SEL_BLOCK <= pos_r
        forced = (j == 0) | (j == cur) | (j == cur - 1)
        score = jnp.where(valid, jnp.where(forced, FORCE_SCORE, p_slc), -jnp.inf)
        sel = _top_blocks(score, j, 1)
        sel = jnp.concatenate([sel] * GROUP_A, axis=0).astype(BF16)

        picked = jnp.dot(sel, e_ref[...], preferred_element_type=F32)
        s_s = jnp.dot(q, hist[0:KV_W, :].astype(BF16), preferred_element_type=F32)
        pos_k = lax.broadcasted_iota(jnp.int32, (1, n_keys), 1)
        p = _masked_softmax(s_s, (picked > 0.5) & (pos_k <= pos_q)).astype(BF16)
        o_sel = diag(lax.dot_general(p, hist[KV_W:, :].astype(BF16), (((1,), (1,)), ((), ())),
                                     preferred_element_type=F32))

        wk = wkv_ref[:, 0:KV_W].astype(BF16)
        s_w = lax.dot_general(q, wk, (((1,), (1,)), ((), ())), preferred_element_type=F32)
        pos_w = past - WINDOW + lax.broadcasted_iota(jnp.int32, (1, n_win), 1)
        p = _masked_softmax(s_w, (pos_w <= pos_q) & (pos_q - pos_w < WINDOW)).astype(BF16)
        o_win = diag(jnp.dot(p, wkv_ref[:, KV_W:].astype(BF16), preferred_element_type=F32))

        gates = jax.nn.sigmoid(g_ref[...])
        o = gates[:, 0:1] * o_cmp + gates[:, 1:2] * o_sel + gates[:, 2:3] * o_win
        o_ref[...] = o * _silu(ga_ref[...])


def _nsa_sample(sel_view, table, qbs, ckt, cv, new_sel, wkv, g_rows, ga_rows, *, past, t_new):
    NB, n_pages = table.shape
    n_steps = n_pages // PAGES_PER_STEP
    n_keys = past + PAGE_SIZE
    n_cmp = ckt.shape[2]
    n_blk = 2 * LANES
    n_rows = qbs.shape[1]
    e = _block_expand(n_blk, n_keys)
    ov = _overlap(n_cmp, n_blk)

    def page_spec(r):
        return pl.BlockSpec((None, KV_BRANCH, PAGE_SIZE), lambda b, s, tab: (tab[b, s * PAGES_PER_STEP + r], 0, 0))

    def per_b(arr):
        return pl.BlockSpec((None,) + arr.shape[1:], lambda b, s, tab: (b, 0, 0))

    const2 = lambda b, s, tab: (0, 0)
    return pl.pallas_call(
        functools.partial(_nsa_sample_kernel, past=past, n_steps=n_steps, t_new=t_new),
        out_shape=jax.ShapeDtypeStruct((NB, n_rows, HEAD_DIM_A), F32),
        grid_spec=pltpu.PrefetchScalarGridSpec(
            num_scalar_prefetch=1,
            grid=(NB, n_steps),
            in_specs=[page_spec(r) for r in range(PAGES_PER_STEP)] + [
                per_b(qbs), per_b(ckt), per_b(cv), per_b(new_sel), per_b(wkv), per_b(g_rows), per_b(ga_rows),
                pl.BlockSpec(e.shape, const2, pipeline_mode=pl.Buffered(1)),
                pl.BlockSpec(ov.shape, const2, pipeline_mode=pl.Buffered(1))],
            out_specs=pl.BlockSpec((None, n_rows, HEAD_DIM_A), lambda b, s, tab: (b, 0, 0)),
            scratch_shapes=[pltpu.VMEM((KV_BRANCH, n_keys), F32)]),
        compiler_params=_cparams(("parallel", "arbitrary")),
        name="nsa_sample",
    )(table, *([sel_view] * PAGES_PER_STEP), qbs, ckt, cv, new_sel, wkv, g_rows, ga_rows, e, ov)


def _permute_w_in_a(w):
    c_kv, c_g, c_ga, c_glu, c_gb = 1024, 2560, 2608, 3632, 5680
    cols = [w[:, :c_kv], w[:, c_ga:c_glu], w[:, c_glu:c_gb], w[:, c_gb:], w[:, c_kv:c_g], w[:, c_g:c_ga]]
    n = sum(c.shape[1] for c in cols)
    return jnp.concatenate(cols + [jnp.zeros((w.shape[0], EVEN_N - n), w.dtype)], axis=1).astype(BF16)


def _pad_rows(x, front, total):
    return jnp.pad(x, ((0, 0), (front, total - front - x.shape[1]), (0, 0)))


def _even_prompt(z, cmpw, conv_args, *, tq):
    B, T, _ = z.shape
    kv = z[:, :, COL_KV:COL_G]
    kv_cmp, kv_sel, kv_win = (kv[:, :, br * KV_BRANCH:(br + 1) * KV_BRANCH] for br in range(3))
    n_pages = T // PAGE_SIZE
    table = jnp.arange(B * n_pages, dtype=jnp.int32).reshape(B, n_pages)
    ckv = _compress(kv_cmp.reshape(B * n_pages, PAGE_SIZE, KV_BRANCH), table, *cmpw, transposed=False)

    def split_kv(x):
        return x[:, :, :KV_W].astype(BF16), _values_t(x[:, :, KV_W:])

    o_a = _nsa_prompt(z, *split_kv(ckv), *split_kv(kv_sel), *split_kv(kv_win))
    u = _glu(z, tt=tq)
    halo_map = lambda bb, i: (bb, jnp.maximum(i * (tq // CONV_HALO) - 1, 0), 0)
    o_b = _conv(u, halo_map, True, u, z, COL_GB // D_MODEL, *conv_args, tt=tq)
    kv_shape = (B, T, 2, N_KV_A, HEAD_DIM_A)
    w_keep = min(WINDOW, T)
    outs = (kv_cmp.reshape(kv_shape), kv_sel.reshape(kv_shape),
            kv_win[:, T - w_keep:].reshape(B, w_keep, 2, N_KV_A, HEAD_DIM_A), u[:, T - (CONV_WIDTH - 1):])
    return o_a, o_b, outs


def _even_sample(z, cmpw, conv_args, cmp_view, sel_view, table, win_buf, conv_hist, *, past):
    NB, t_new, _ = z.shape
    kv = z[:, :, COL_KV:COL_G]
    kv_cmp, kv_sel, kv_win = (kv[:, :, br * KV_BRANCH:(br + 1) * KV_BRANCH] for br in range(3))
    ckv = _compress(cmp_view, table, *cmpw, transposed=True)
    q = z[:, :, COL_Q:COL_Q + D_A].reshape(NB, t_new, N_KV_A, GROUP_A, HEAD_DIM_A) * (HEAD_DIM_A ** -0.5)
    qbs = jnp.einsum('bqkgd,kK->bgkqKd', q, jnp.eye(N_KV_A, dtype=F32)).reshape(NB, N_HEADS_A * t_new, KV_W)
    g_rows = z[:, :, COL_G:COL_G + 3 * N_HEADS_A].reshape(NB, t_new, 3, N_KV_A, GROUP_A)
    g_rows = jnp.transpose(g_rows, (0, 4, 3, 1, 2)).reshape(NB, N_HEADS_A * t_new, 3)
    g_rows = jnp.pad(g_rows, ((0, 0), (0, 0), (0, LANES - 3)))
    ga_rows = z[:, :, COL_GA:COL_GA + D_A].reshape(NB, t_new, N_KV_A, GROUP_A, HEAD_DIM_A)
    ga_rows = jnp.transpose(ga_rows, (0, 3, 2, 1, 4)).reshape(NB, N_HEADS_A * t_new, HEAD_DIM_A)
    wkv = jnp.concatenate([win_buf.reshape(NB, -1, KV_BRANCH), kv_win], axis=1)
    wb = win_buf.shape[1]
    o = _nsa_sample(sel_view, table, qbs.astype(BF16),
                    jnp.swapaxes(ckv[:, :, :KV_W], 1, 2).astype(BF16), ckv[:, :, KV_W:].astype(BF16),
                    jnp.swapaxes(_pad_rows(kv_sel, 0, PAGE_SIZE), 1, 2), _pad_rows(wkv, 0, WIN_KEYS), g_rows, ga_rows,
                    past=past, t_new=t_new)
    o_a = jnp.transpose(o.reshape(NB, GROUP_A, N_KV_A, t_new, HEAD_DIM_A), (0, 3, 2, 1, 4))
    o_a = o_a.reshape(NB * t_new, D_A).astype(BF16)
    u = _glu(z.reshape(1, NB * t_new, EVEN_N), tt=NB * t_new).reshape(NB, t_new, D_MODEL)
    halo = _pad_rows(conv_hist, CONV_HALO - (CONV_WIDTH - 1), CONV_HALO)
    gate_b = _pad_rows(z[:, :, COL_GB:COL_GB + D_MODEL], 0, SUBLANES)
    o_b = _conv(halo, lambda bb, i: (bb, 0, 0), False, _pad_rows(u, 0, SUBLANES), gate_b, 0, *conv_args, tt=SUBLANES)
    o_b = o_b[:, :t_new].reshape(NB * t_new, D_MODEL)
    kv_shape = (NB, t_new, 2, N_KV_A, HEAD_DIM_A)
    outs = (kv_cmp.reshape(kv_shape), kv_sel.reshape(kv_shape),
            wkv[:, t_new:].reshape(NB, wb, 2, N_KV_A, HEAD_DIM_A),
            jnp.concatenate([conv_hist, u], axis=1)[:, -(CONV_WIDTH - 1):])
    return o_a, o_b, outs


def kernel(x_prompt, x_sample, mem_prompt, cache_cmp_kv, cache_sel_kv, cache_win_kv, state_conv, state_pool,
           cache_mem_kv, page_table, norm_mix, norm_mem, norm_out, w_in_a, w_out_a, cmp_pe, cmp_w1, cmp_w2,
           conv_w, conv_b, conv_ln_g, conv_ln_b, w_in_c, pool_w, pool_scale, w_out_c, w_mem_q, w_mem_kv, w_mem_o):
    B, T, D = x_prompt.shape
    NB, t_new, _ = x_sample.shape
    depth = norm_mix.shape[0]
    n_pool = cache_cmp_kv.shape[1]
    past = page_table.shape[1] * PAGE_SIZE
    mem_len = mem_prompt.shape[1]
    tm_p = 1024
    tq = 256
    ms = NB * t_new
    xp = x_prompt.reshape(B * T, D)
    xs = x_sample.reshape(ms, D)
    p_cmp, p_sel, p_win, p_conv, p_pool, p_mem = [], [], [], [], [], []
    s_cmp, s_sel, s_win, s_conv, s_pool = [], [], [], [], []

    def pages_t(cache):
        return jnp.transpose(cache, (0, 1, 3, 4, 5, 2)).reshape(-1, KV_BRANCH, PAGE_SIZE)

    cmp_view = pages_t(cache_cmp_kv)
    sel_view = pages_t(cache_sel_kv)

    for layer in range(depth):
        if layer % 2 == 0:
            a = layer // 2
            w_in = _permute_w_in_a(w_in_a[a])
            w_out = w_out_a[a].astype(BF16)
            cmpw = _compress_weights(cmp_pe[a], cmp_w1[a], cmp_w2[a])
            conv_args = (conv_w[a], conv_b[a], conv_ln_g[a], conv_ln_b[a])
            zp = _mm([xp], w_in, gain=norm_mix[layer], tm=tm_p, tn=EVEN_TN).reshape(B, T, EVEN_N)
            o_a, o_b, outs = _even_prompt(zp, cmpw, conv_args, tq=tq)
            xp = _mm([o_a.reshape(B * T, D_A), o_b.reshape(B * T, D)], w_out, res=xp, tm=tm_p, tn=D)
            for lst, val in zip((p_cmp, p_sel, p_win, p_conv), outs):
                lst.append(val)
            zs = _mm([xs], w_in, gain=norm_mix[layer], tm=ms, tn=EVEN_TN).reshape(NB, t_new, EVEN_N)
            table = page_table + a * n_pool
            o_a, o_b, outs = _even_sample(zs, cmpw, conv_args, cmp_view, sel_view, table, cache_win_kv[a],
                                          state_conv[a], past=past)
            xs = _mm([o_a, o_b], w_out, res=xs, tm=ms, tn=D)
            for lst, val in zip((s_cmp, s_sel, s_win, s_conv), outs):
                lst.append(val)
        else:
            c = layer // 2
            w_in = w_in_c[c].astype(BF16)
            w_out = w_out_c[c].astype(BF16)
            zp = _mm([xp], w_in, gain=norm_mix[layer], tm=tm_p, tn=D).reshape(B, T, 2 * D)
            halo_map = lambda bb, i: (bb, jnp.maximum(i * (tq // POOL_HALO) - 1, 0), 0)
            m = _pool(zp, halo_map, True, zp, pool_w[c], pool_scale[c], tt=tq, pos0=0)
            xp = _mm([m.reshape(B * T, D)], w_out, res=xp, tm=tm_p, tn=D)
            p_pool.append(zp[:, T - POOL_HIST:, :D])
            zs = _mm([xs], w_in, gain=norm_mix[layer], tm=ms, tn=D).reshape(NB, t_new, 2 * D)
            hist = state_pool[c]
            m = _pool(_pad_rows(hist, POOL_HALO - POOL_HIST, POOL_HALO), lambda bb, i: (bb, 0, 0), False,
                      _pad_rows(zs, 0, SUBLANES), pool_w[c], pool_scale[c], tt=SUBLANES, pos0=past)
            xs = _mm([m[:, :t_new].reshape(ms, D)], w_out, res=xs, tm=ms, tn=D)
            s_pool.append(jnp.concatenate([hist, zs[:, :, :D]], axis=1)[:, -POOL_HIST:])

        wq = w_mem_q[layer].astype(BF16)
        wo = w_mem_o[layer].astype(BF16)
        mkv = _mm([mem_prompt.reshape(B * mem_len, D)], w_mem_kv[layer].astype(BF16), tm=B * mem_len, tn=D)
        mkv = mkv.reshape(B, mem_len, 2 * D)
        p_mem.append(mkv.reshape(B, mem_len, 2, N_HEADS_M, HEAD_DIM_M))
        q = _mm([xp], wq, gain=norm_mem[layer], tm=tm_p, tn=D, out_dtype=BF16).reshape(B, T, D)
        o = _mem_attn(q, jnp.swapaxes(mkv[:, :, :D], 1, 2).astype(BF16), mkv[:, :, D:].astype(BF16), tm=tq)
        xp = _mm([o.reshape(B * T, D)], wo, res=xp, tm=tm_p, tn=D)
        ckv = cache_mem_kv[layer].reshape(NB, mem_len, 2 * D)
        q = _mm([xs], wq, gain=norm_mem[layer], tm=ms, tn=D, out_dtype=BF16).reshape(NB, t_new, D)
        o = _mem_attn(_pad_rows(q, 0, SUBLANES), jnp.swapaxes(ckv[:, :, :D], 1, 2).astype(BF16),
                      ckv[:, :, D:].astype(BF16), tm=SUBLANES)
        xs = _mm([o[:, :t_new].reshape(ms, D)], wo, res=xs, tm=ms, tn=D)

    y_prompt = _rmsnorm(xp, norm_out, tm=tm_p).reshape(B, T, D)
    y_sample = _rmsnorm(xs, norm_out, tm=ms).reshape(NB, t_new, D)
    return (y_prompt, y_sample,
            jnp.stack(p_cmp), jnp.stack(p_sel), jnp.stack(p_win), jnp.stack(p_conv), jnp.stack(p_pool),
            jnp.stack(p_mem),
            jnp.stack(s_cmp), jnp.stack(s_sel), jnp.stack(s_win), jnp.stack(s_conv), jnp.stack(s_pool))
```

```python
import functools

import jax
import jax.numpy as jnp
from jax import lax
from jax.experimental import pallas as pl
from jax.experimental.pallas import tpu as pltpu

F32 = jnp.float32
BF16 = jnp.bfloat16

LANES = 128
SUBLANES = 8
VMEM_LIMIT_BYTES = 56 * 1024 * 1024

D_MODEL = 1024
N_HEADS_A = 16
HEAD_DIM_A = 64
N_KV_A = 4
GROUP_A = N_HEADS_A // N_KV_A
D_A = N_HEADS_A * HEAD_DIM_A
KV_W = N_KV_A * HEAD_DIM_A
KV_BRANCH = 2 * KV_W
CMP_STRIDE = 16
CMP_BLOCK = 32
CMP_HIDDEN = 128
SEL_BLOCK = 64
N_SEL = 16
WINDOW = 512
Q_BLOCK = 128
FORCE_SCORE = 1000.0
CONV_WIDTH = 31
CONV_HALO = 32
POOL_WINDOWS = (2, 4, 8, 16)
POOL_GROUP = D_MODEL // len(POOL_WINDOWS)
POOL_HIST = 15
POOL_HALO = 16
PAGE_SIZE = 128
N_HEADS_M = 4
HEAD_DIM_M = 256
NORM_EPS = 1e-6
NEG = -1e30

COL_Q, COL_GA, COL_VAL, COL_GL, COL_GB = 0, 1024, 2048, 3072, 4096
COL_KV = 5120
COL_G = COL_KV + 3 * KV_BRANCH
EVEN_N = 6912
EVEN_TN = 768
PAGES_PER_STEP = 8


def _cparams(sem):
    return pltpu.CompilerParams(dimension_semantics=sem, vmem_limit_bytes=VMEM_LIMIT_BYTES)


def _silu(x):
    return x * jax.nn.sigmoid(x)


def _mm_kernel(*refs, n_parts, has_gain, has_res):
    parts = refs[:n_parts]
    pos = n_parts
    gain_ref = refs[pos] if has_gain else None
    pos += int(has_gain)
    w_ref = refs[pos]
    pos += 1
    res_ref = refs[pos] if has_res else None
    pos += int(has_res)
    o_ref, lhs = refs[pos], refs[pos + 1]

    @pl.when(pl.program_id(1) == 0)
    def _():
        off = 0
        for p in parts:
            x = p[...].astype(F32)
            if has_gain:
                x = x * lax.rsqrt(jnp.mean(x * x, axis=-1, keepdims=True) + NORM_EPS) * gain_ref[...]
            lhs[:, off:off + p.shape[1]] = x.astype(BF16)
            off += p.shape[1]

    acc = jnp.dot(lhs[...], w_ref[...], preferred_element_type=F32)
    if has_res:
        acc = acc + res_ref[...]
    o_ref[...] = acc.astype(o_ref.dtype)


def _mm(parts, w, *, gain=None, res=None, tm, tn, out_dtype=F32):
    M = parts[0].shape[0]
    K, N = w.shape
    assert sum(p.shape[1] for p in parts) == K and M % tm == 0 and N % tn == 0
    in_specs = [pl.BlockSpec((tm, p.shape[1]), lambda i, j: (i, 0)) for p in parts]
    args = list(parts)
    if gain is not None:
        in_specs.append(pl.BlockSpec((1, K), lambda i, j: (0, 0)))
        args.append(gain.reshape(1, K).astype(F32))
    in_specs.append(pl.BlockSpec((K, tn), lambda i, j: (0, j)))
    args.append(w)
    if res is not None:
        in_specs.append(pl.BlockSpec((tm, tn), lambda i, j: (i, j)))
        args.append(res)
    return pl.pallas_call(
        functools.partial(_mm_kernel, n_parts=len(parts), has_gain=gain is not None, has_res=res is not None),
        out_shape=jax.ShapeDtypeStruct((M, N), out_dtype),
        grid=(M // tm, N // tn),
        in_specs=in_specs,
        out_specs=pl.BlockSpec((tm, tn), lambda i, j: (i, j)),
        scratch_shapes=[pltpu.VMEM((tm, K), BF16)],
        compiler_params=_cparams(("parallel", "arbitrary")),
        name="mm",
    )(*args)


def _rmsnorm_kernel(x_ref, g_ref, o_ref):
    x = x_ref[...]
    o_ref[...] = x * lax.rsqrt(jnp.mean(x * x, axis=-1, keepdims=True) + NORM_EPS) * g_ref[...]


def _rmsnorm(x, g, *, tm):
    M, D = x.shape
    return pl.pallas_call(
        _rmsnorm_kernel,
        out_shape=jax.ShapeDtypeStruct((M, D), F32),
        grid=(M // tm,),
        in_specs=[pl.BlockSpec((tm, D), lambda i: (i, 0)), pl.BlockSpec((1, D), lambda i: (0, 0))],
        out_specs=pl.BlockSpec((tm, D), lambda i: (i, 0)),
        compiler_params=_cparams(("parallel",)),
        name="rmsnorm",
    )(x, g.reshape(1, D))


def _glu_kernel(val_ref, gl_ref, o_ref):
    o_ref[...] = val_ref[...] * jax.nn.sigmoid(gl_ref[...])


def _glu(z, *, tt):
    B, T, _ = z.shape
    blk = (None, tt, D_MODEL)
    return pl.pallas_call(
        _glu_kernel,
        out_shape=jax.ShapeDtypeStruct((B, T, D_MODEL), F32),
        grid=(B, T // tt),
        in_specs=[pl.BlockSpec(blk, lambda b, i: (b, i, COL_VAL // D_MODEL)),
                  pl.BlockSpec(blk, lambda b, i: (b, i, COL_GL // D_MODEL))],
        out_specs=pl.BlockSpec(blk, lambda b, i: (b, i, 0)),
        compiler_params=_cparams(("parallel", "parallel")),
        name="glu",
    )(z, z)


CONV_ROWS = 32


def _conv_kernel(halo_ref, cur_ref, gate_ref, w_ref, b_ref, lg_ref, lb_ref, o_ref, buf, sh, y_scr, *, tt, zero_first_halo):
    halo = halo_ref[...]
    if zero_first_halo:
        halo = jnp.where(pl.program_id(1) == 0, 0.0, halo)
    buf[0:CONV_HALO, :] = halo
    buf[CONV_HALO:CONV_HALO + tt, :] = cur_ref[...]
    buf[CONV_HALO + tt:, :] = jnp.zeros((SUBLANES, D_MODEL), F32)
    first = CONV_HALO - (CONV_WIDTH - 1)
    n_sh = sh.shape[1]
    for r in range(SUBLANES):
        sh[r] = buf[first + r:first + r + n_sh, :]
    rc = min(CONV_ROWS, tt)
    for r0 in range(0, tt, rc):
        acc = jnp.broadcast_to(b_ref[...], (rc, D_MODEL))
        for j in range(CONV_WIDTH):
            a0 = r0 + (j // SUBLANES) * SUBLANES
            acc = acc + w_ref[j:j + 1, :] * sh[j % SUBLANES, a0:a0 + rc, :]
        y_scr[r0:r0 + rc, :] = acc
    y = y_scr[...]
    xc = y - jnp.mean(y, axis=-1, keepdims=True)
    yn = xc * lax.rsqrt(jnp.mean(xc * xc, axis=-1, keepdims=True) + NORM_EPS) * lg_ref[...] + lb_ref[...]
    o_ref[...] = (_silu(yn) * _silu(gate_ref[...])).astype(o_ref.dtype)


def _conv(halo_arr, halo_map, zero_first_halo, u, gate_arr, gate_col, w, b, lg, lb, *, tt):
    B, T, _ = u.shape
    w32 = jnp.concatenate([w, jnp.zeros((CONV_HALO - CONV_WIDTH + 1, D_MODEL), F32)], axis=0)[:CONV_HALO]
    vec = pl.BlockSpec((1, D_MODEL), lambda bb, i: (0, 0))
    return pl.pallas_call(
        functools.partial(_conv_kernel, tt=tt, zero_first_halo=zero_first_halo),
        out_shape=jax.ShapeDtypeStruct((B, T, D_MODEL), BF16),
        grid=(B, T // tt),
        in_specs=[pl.BlockSpec((None, CONV_HALO, D_MODEL), halo_map),
                  pl.BlockSpec((None, tt, D_MODEL), lambda bb, i: (bb, i, 0)),
                  pl.BlockSpec((None, tt, D_MODEL), lambda bb, i: (bb, i, gate_col)),
                  pl.BlockSpec((CONV_HALO, D_MODEL), lambda bb, i: (0, 0)),
                  vec, vec, vec],
        out_specs=pl.BlockSpec((None, tt, D_MODEL), lambda bb, i: (bb, i, 0)),
        scratch_shapes=[pltpu.VMEM((CONV_HALO + tt + SUBLANES, D_MODEL), F32),
                        pltpu.VMEM((SUBLANES, tt + CONV_HALO - SUBLANES, D_MODEL), F32),
                        pltpu.VMEM((tt, D_MODEL), F32)],
        compiler_params=_cparams(("parallel", "parallel")),
        name="conv",
    )(halo_arr, u, gate_arr, w32, b.reshape(1, -1), lg.reshape(1, -1), lb.reshape(1, -1))


def _pool_kernel(halo_ref, cur_ref, gate_ref, pw_ref, ps_ref, o_ref, buf, *, tt, zero_first_halo, pos0):
    i = pl.program_id(1)
    halo = halo_ref[...]
    if zero_first_halo:
        halo = jnp.where(i == 0, 0.0, halo)
    buf[0:POOL_HALO, :] = halo
    buf[POOL_HALO:POOL_HALO + tt, :] = cur_ref[...]
    pos = pos0 + i * tt + lax.broadcasted_iota(jnp.int32, (tt, 1), 0)
    for g, w in enumerate(POOL_WINDOWS):
        c0, c1 = g * POOL_GROUP, (g + 1) * POOL_GROUP
        s = buf[POOL_HALO:POOL_HALO + tt, c0:c1]
        for j in range(1, w):
            s = s + buf[POOL_HALO - j:POOL_HALO - j + tt, c0:c1]
        cnt = jnp.minimum(w, pos + 1).astype(F32)
        d = s / cnt - buf[POOL_HALO:POOL_HALO + tt, c0:c1]
        m = jnp.dot(d.astype(BF16), pw_ref[g], preferred_element_type=F32)
        o_ref[:, c0:c1] = (m * ps_ref[:, c0:c1] * _silu(gate_ref[:, c0:c1])).astype(o_ref.dtype)


def _pool(halo_arr, halo_map, zero_first_halo, z, pool_w, pool_scale, *, tt, pos0):
    B, T, _ = z.shape
    return pl.pallas_call(
        functools.partial(_pool_kernel, tt=tt, zero_first_halo=zero_first_halo, pos0=pos0),
        out_shape=jax.ShapeDtypeStruct((B, T, D_MODEL), BF16),
        grid=(B, T // tt),
        in_specs=[pl.BlockSpec((None, POOL_HALO, D_MODEL), halo_map),
                  pl.BlockSpec((None, tt, D_MODEL), lambda bb, i: (bb, i, 0)),
                  pl.BlockSpec((None, tt, D_MODEL), lambda bb, i: (bb, i, 1)),
                  pl.BlockSpec((len(POOL_WINDOWS), POOL_GROUP, POOL_GROUP), lambda bb, i: (0, 0, 0)),
                  pl.BlockSpec((1, D_MODEL), lambda bb, i: (0, 0))],
        out_specs=pl.BlockSpec((None, tt, D_MODEL), lambda bb, i: (bb, i, 0)),
        scratch_shapes=[pltpu.VMEM((POOL_HALO + tt, D_MODEL), F32)],
        compiler_params=_cparams(("parallel", "parallel")),
        name="pool",
    )(halo_arr, z, z, pool_w.astype(BF16), pool_scale.reshape(1, -1))


def _mem_attn_kernel(q_ref, kt_ref, v_ref, o_ref):
    for h in range(N_HEADS_M):
        c0, c1 = h * HEAD_DIM_M, (h + 1) * HEAD_DIM_M
        q = q_ref[:, c0:c1] * (HEAD_DIM_M ** -0.5)
        s = jnp.dot(q, kt_ref[c0:c1, :], preferred_element_type=F32)
        e = jnp.exp(s - jnp.max(s, axis=-1, keepdims=True))
        p = e / jnp.sum(e, axis=-1, keepdims=True)
        o = jnp.dot(p.astype(BF16), v_ref[:, c0:c1], preferred_element_type=F32)
        o_ref[:, c0:c1] = o.astype(o_ref.dtype)


def _mem_attn(q, kt, v, *, tm):
    NB, Tq, D = q.shape
    M = v.shape[1]
    return pl.pallas_call(
        _mem_attn_kernel,
        out_shape=jax.ShapeDtypeStruct((NB, Tq, D), BF16),
        grid=(NB, Tq // tm),
        in_specs=[pl.BlockSpec((None, tm, D), lambda b, i: (b, i, 0)),
                  pl.BlockSpec((None, D, M), lambda b, i: (b, 0, 0)),
                  pl.BlockSpec((None, M, D), lambda b, i: (b, 0, 0))],
        out_specs=pl.BlockSpec((None, tm, D), lambda b, i: (b, i, 0)),
        compiler_params=_cparams(("parallel", "parallel")),
        name="mem_attn",
    )(q, kt, v)


def _compress_weights(pe, w1, w2):
    eye2 = jnp.eye(2, dtype=F32)

    def bd(wc):
        return jnp.einsum('ldh,kK->lkdKh', wc, eye2).reshape(CMP_STRIDE * LANES, 2 * CMP_HIDDEN)

    w1bd = jnp.stack([jnp.concatenate([bd(w1[c, :CMP_STRIDE]), bd(w1[c, CMP_STRIDE:])], axis=1)
                      for c in range(2)]).astype(BF16)
    w2bd = jnp.stack([jnp.einsum('hd,kK->khKd', w2[c], eye2).reshape(2 * CMP_HIDDEN, LANES)
                      for c in range(2)]).astype(BF16)
    pe_rows = jnp.zeros((2, 2 * SUBLANES, CMP_BLOCK * HEAD_DIM_A), F32).at[:, 0].set(pe.reshape(2, -1))
    bias = jnp.stack([_mm([pe_rows[c]], w1[c].reshape(CMP_BLOCK * HEAD_DIM_A, CMP_HIDDEN).astype(BF16),
                          tm=2 * SUBLANES, tn=CMP_HIDDEN)[0] for c in range(2)])
    bias2 = jnp.concatenate([bias, bias], axis=1).reshape(2, 1, 2 * CMP_HIDDEN)
    return w1bd, w2bd, bias2


def _compress_kernel(tab_ref, *refs, n_pos, n_steps, transposed):
    pages = refs[:PAGES_PER_STEP]
    w1_ref, bias_ref, w2_ref, o_ref, hist, sec = refs[PAGES_PER_STEP:]
    s = pl.program_id(1)
    n_grp = KV_BRANCH // LANES
    per_page = PAGE_SIZE // CMP_STRIDE
    n_rows = n_pos // CMP_STRIDE
    out_row = lax.broadcasted_iota(jnp.int32, (PAGE_SIZE, PAGE_SIZE), 0)
    in_pos = lax.broadcasted_iota(jnp.int32, (PAGE_SIZE, PAGE_SIZE), 1)
    perm = (in_pos == (out_row % per_page) * CMP_STRIDE + out_row // per_page).astype(BF16)
    for r in range(PAGES_PER_STEP):
        row = pl.multiple_of((s * PAGES_PER_STEP + r) * per_page, per_page)
        page = pages[r][...].astype(BF16)
        if transposed:
            x = lax.dot_general(perm, page, (((1,), (1,)), ((), ())), preferred_element_type=F32)
        else:
            x = jnp.dot(perm, page, preferred_element_type=F32)
        for grp in range(n_grp):
            for l in range(CMP_STRIDE):
                hist[grp, l, pl.ds(row, per_page), :] = x[l * per_page:(l + 1) * per_page,
                                                          grp * LANES:(grp + 1) * LANES]

    @pl.when(s == 0)
    def _():
        hist[:, :, n_rows:, :] = jnp.zeros((n_grp, CMP_STRIDE, hist.shape[2] - n_rows, LANES), F32)

    @pl.when(s == n_steps - 1)
    def _():
        for grp in range(n_grp):
            c = grp // 2
            xg = jnp.concatenate([hist[grp, l] for l in range(CMP_STRIDE)], axis=1).astype(BF16)
            fs = jnp.dot(xg, w1_ref[c], preferred_element_type=F32)
            sec[...] = fs[:, 2 * CMP_HIDDEN:]
            hid = _silu(fs[0:n_rows, :2 * CMP_HIDDEN] + sec[1:n_rows + 1, :] + bias_ref[c])
            o_ref[:, grp * LANES:(grp + 1) * LANES] = jnp.dot(hid.astype(BF16), w2_ref[c],
                                                              preferred_element_type=F32)


def _compress(pages, table, w1bd, w2bd, bias2, *, transposed):
    NB, n_pages = table.shape
    n_steps = n_pages // PAGES_PER_STEP
    n_pos = n_pages * PAGE_SIZE
    n_rows = n_pos // CMP_STRIDE

    def page_spec(r):
        return pl.BlockSpec((None,) + pages.shape[1:], lambda b, s, tab: (tab[b, s * PAGES_PER_STEP + r], 0, 0))

    const3 = lambda b, s, tab: (0, 0, 0)
    return pl.pallas_call(
        functools.partial(_compress_kernel, n_pos=n_pos, n_steps=n_steps, transposed=transposed),
        out_shape=jax.ShapeDtypeStruct((NB, n_rows, KV_BRANCH), F32),
        grid_spec=pltpu.PrefetchScalarGridSpec(
            num_scalar_prefetch=1,
            grid=(NB, n_steps),
            in_specs=[page_spec(r) for r in range(PAGES_PER_STEP)] + [
                pl.BlockSpec(w1bd.shape, const3), pl.BlockSpec(bias2.shape, const3), pl.BlockSpec(w2bd.shape, const3)],
            out_specs=pl.BlockSpec((None, n_rows, KV_BRANCH), lambda b, s, tab: (b, 0, 0)),
            scratch_shapes=[pltpu.VMEM((KV_BRANCH // LANES, CMP_STRIDE, n_rows + SUBLANES, LANES), F32),
                            pltpu.VMEM((n_rows + SUBLANES, 2 * CMP_HIDDEN), F32)]),
        compiler_params=_cparams(("parallel", "arbitrary")),
        name="compress",
    )(table, *([pages] * PAGES_PER_STEP), w1bd, bias2, w2bd)


def _overlap(n_cmp_rows, n_sel_cols):
    c0 = jnp.arange(n_cmp_rows) * CMP_STRIDE
    s0 = jnp.arange(n_sel_cols) * SEL_BLOCK
    ov = jnp.minimum(c0[:, None] + CMP_BLOCK, s0[None, :] + SEL_BLOCK) - jnp.maximum(c0[:, None], s0[None, :])
    return (jnp.clip(ov, 0, None).astype(F32) / CMP_BLOCK).astype(BF16)


def _block_expand(n_blocks, n_keys):
    return (jnp.arange(n_keys)[None, :] // SEL_BLOCK == jnp.arange(n_blocks)[:, None]).astype(BF16)


def _masked_softmax(s, mask, axis=-1):
    m = jnp.max(jnp.where(mask, s, NEG), axis=axis, keepdims=True)
    e = jnp.where(mask, jnp.exp(s - m), 0.0)
    d = jnp.sum(e, axis=axis, keepdims=True)
    return e / jnp.where(d > 0, d, 1.0)


def _top_blocks(score, idx, axis):
    n = score.shape[axis]
    sel = jnp.zeros(score.shape, F32)
    work = score
    for _ in range(N_SEL):
        m = jnp.max(work, axis=axis, keepdims=True)
        first = jnp.min(jnp.where(work == m, idx, n), axis=axis, keepdims=True)
        pick = idx == first
        sel = jnp.where(pick & (m > -jnp.inf), 1.0, sel)
        work = jnp.where(pick, -jnp.inf, work)
    return sel


SEL_TK = 512
WIN_KEYS = WINDOW + Q_BLOCK
LOG2E = 1.4426950408889634
V_ROWS = HEAD_DIM_A + 16


def _kv_split_kernel(kv_ref, k_ref, vt_ref):
    k_ref[...] = kv_ref[:, :KV_W].astype(BF16)
    vt = kv_ref[:, KV_W:].T
    tt = vt.shape[1]
    pad_rows = lax.broadcasted_iota(jnp.int32, (V_ROWS - HEAD_DIM_A, tt), 0)
    ones_then_zeros = jnp.where(pad_rows == 0, 1.0, 0.0).astype(BF16)
    for k in range(N_KV_A):
        vt_ref[k * V_ROWS:k * V_ROWS + HEAD_DIM_A, :] = vt[k * HEAD_DIM_A:(k + 1) * HEAD_DIM_A, :].astype(BF16)
        vt_ref[k * V_ROWS + HEAD_DIM_A:(k + 1) * V_ROWS, :] = ones_then_zeros


def _kv_split(z, col, *, tt):
    B, T, _ = z.shape
    return pl.pallas_call(
        _kv_split_kernel,
        out_shape=(jax.ShapeDtypeStruct((B, T, KV_W), BF16), jax.ShapeDtypeStruct((B, N_KV_A * V_ROWS, T), BF16)),
        grid=(B, T // tt),
        in_specs=[pl.BlockSpec((None, tt, KV_BRANCH), lambda b, i: (b, i, col // KV_BRANCH))],
        out_specs=(pl.BlockSpec((None, tt, KV_W), lambda b, i: (b, i, 0)),
                   pl.BlockSpec((None, N_KV_A * V_ROWS, tt), lambda b, i: (b, 0, i))),
        compiler_params=_cparams(("parallel", "parallel")),
        name="kv_split",
    )(z)


def _values_t(v):
    B, n, _ = v.shape
    vt = jnp.swapaxes(v, 1, 2).reshape(B, N_KV_A, HEAD_DIM_A, n)
    pad = jnp.zeros((B, N_KV_A, V_ROWS - HEAD_DIM_A, n), F32).at[:, :, 0].set(1.0)
    return jnp.concatenate([vt, pad], axis=2).reshape(B, N_KV_A * V_ROWS, n).astype(BF16)


def _nsa_prompt_kernel(q_ref, ga_ref, g_ref, ck_ref, cvt_ref, sk_ref, svt_ref, wk_ref, wvt_ref, et_ref, ovt_ref,
                       o_ref, qt_scr, ot_scr, qb_scr, m_scr, acc_scr):
    i = pl.program_id(1)
    q0 = i * Q_BLOCK
    n_cmp = ck_ref.shape[0]
    pos_q = q0 + lax.broadcasted_iota(jnp.int32, (1, Q_BLOCK), 1)
    for c in range(D_A // LANES):
        cols = slice(c * LANES, (c + 1) * LANES)
        qt_scr[cols, :] = (q_ref[:, cols].T * (HEAD_DIM_A ** -0.5 * LOG2E)).astype(BF16)
    gates_t = jax.nn.sigmoid(g_ref[...]).T
    j_t = lax.broadcasted_iota(jnp.int32, (LANES, Q_BLOCK), 0)
    cur_t = pos_q // SEL_BLOCK
    valid_t = j_t * SEL_BLOCK <= pos_q
    forced_t = (j_t == 0) | (j_t == cur_t) | (j_t == cur_t - 1)
    cmp_end = lax.broadcasted_iota(jnp.int32, (n_cmp, 1), 0) * CMP_STRIDE + (CMP_BLOCK - 1)
    cbias = jnp.where(cmp_end <= pos_q, 0.0, NEG)
    cvalid = jnp.where(pos_q >= CMP_BLOCK - 1, 1.0, 0.0)
    w0 = pl.multiple_of(jnp.maximum(q0 - WINDOW, 0), Q_BLOCK)
    pos_w = w0 + lax.broadcasted_iota(jnp.int32, (WIN_KEYS, 1), 0)
    wbias = jnp.where((pos_w <= pos_q) & (pos_q - pos_w < WINDOW), 0.0, NEG)
    n_full = q0 // SEL_TK
    zero_half = jnp.zeros((HEAD_DIM_A, Q_BLOCK), BF16)

    def weights_t(s, bias):
        out = []
        for g in range(GROUP_A):
            sg = s[:, g * Q_BLOCK:(g + 1) * Q_BLOCK] + bias
            out.append(jnp.exp2(sg - jnp.max(sg, axis=0, keepdims=True)).astype(BF16))
        return jnp.concatenate(out, axis=1)

    def normalise(o_ext, scale=None):
        d = o_ext[HEAD_DIM_A:HEAD_DIM_A + 1, :]
        inv = 1.0 / jnp.where(d > 0, d, 1.0)
        return inv if scale is None else inv * scale

    for k in range(N_KV_A):
        vr = slice(k * V_ROWS, (k + 1) * V_ROWS)
        pr = slice((k // 2) * LANES, (k // 2 + 1) * LANES)
        blocks = []
        for g in range(GROUP_A):
            h = k * GROUP_A + g
            qh = qt_scr[h * HEAD_DIM_A:(h + 1) * HEAD_DIM_A, :]
            blocks.append(jnp.concatenate([qh, zero_half] if k % 2 == 0 else [zero_half, qh], axis=0))
        qt = jnp.concatenate(blocks, axis=1)

        s = jnp.dot(ck_ref[:, pr], qt, preferred_element_type=F32)
        e = weights_t(s, cbias)
        o_ext = jnp.dot(cvt_ref[vr, :], e, preferred_element_type=F32)
        inv = normalise(o_ext, jnp.concatenate([cvalid] * GROUP_A, axis=1))
        o_cmp = o_ext[:HEAD_DIM_A] * inv
        pj = jnp.dot(ovt_ref[...], e, preferred_element_type=F32) * inv
        p_slc = pj[:, 0:Q_BLOCK]
        for g in range(1, GROUP_A):
            p_slc = p_slc + pj[:, g * Q_BLOCK:(g + 1) * Q_BLOCK]

        score_t = jnp.where(valid_t, jnp.where(forced_t, FORCE_SCORE, p_slc), -jnp.inf)
        sel_bias = ((_top_blocks(score_t, j_t, 0) - 1.0) * (-NEG)).astype(BF16)
        qb = jnp.concatenate([qt, jnp.concatenate([sel_bias] * GROUP_A, axis=1)], axis=0)

        qb_scr[k] = qb

        s = jnp.dot(wk_ref[pl.ds(w0, WIN_KEYS), pr], qt, preferred_element_type=F32)
        o_ext = jnp.dot(wvt_ref[vr, pl.ds(w0, WIN_KEYS)], weights_t(s, wbias), preferred_element_type=F32)
        o_win = o_ext[:HEAD_DIM_A] * normalise(o_ext)

        for g in range(GROUP_A):
            h = k * GROUP_A + g
            lanes = slice(g * Q_BLOCK, (g + 1) * Q_BLOCK)
            ot_scr[h * HEAD_DIM_A:(h + 1) * HEAD_DIM_A, :] = (
                gates_t[h:h + 1, :] * o_cmp[:, lanes]
                + gates_t[2 * N_HEADS_A + h:2 * N_HEADS_A + h + 1, :] * o_win[:, lanes])

    m_scr[...] = jnp.full(m_scr.shape, NEG, F32)
    acc_scr[...] = jnp.zeros(acc_scr.shape, F32)

    def sel_step(t, causal):
        k0 = pl.multiple_of(t * SEL_TK, SEL_TK)
        onehot = et_ref[pl.ds(k0, SEL_TK), :]
        if causal:
            pos_k = k0 + lax.broadcasted_iota(jnp.int32, (SEL_TK, 1), 0)
            future = jnp.where(pos_k <= pos_q, 0.0, NEG)

        def scores(k):
            pr = slice((k // 2) * LANES, (k // 2 + 1) * LANES)
            kb = jnp.concatenate([sk_ref[pl.ds(k0, SEL_TK), pr], onehot], axis=1)
            return jnp.dot(kb, qb_scr[k], preferred_element_type=F32)

        s_next = scores(0)
        for k in range(N_KV_A):
            s = s_next
            if k + 1 < N_KV_A:
                s_next = scores(k + 1)
            es, alphas = [], []
            for g in range(GROUP_A):
                lanes = slice(g * Q_BLOCK, (g + 1) * Q_BLOCK)
                sg = s[:, lanes] + future if causal else s[:, lanes]
                m_old = m_scr[k, :, lanes]
                m_new = jnp.maximum(m_old, jnp.max(sg, axis=0, keepdims=True))
                alphas.append(jnp.exp2(m_old - m_new))
                es.append(jnp.exp2(sg - m_new).astype(BF16))
                m_scr[k, :, lanes] = m_new
            pv = jnp.dot(svt_ref[k * V_ROWS:(k + 1) * V_ROWS, pl.ds(k0, SEL_TK)], jnp.concatenate(es, axis=1),
                         preferred_element_type=F32)
            acc_scr[k] = jnp.concatenate(alphas, axis=1) * acc_scr[k] + pv

    def full_step(t, carry):
        sel_step(t, False)
        return carry

    lax.fori_loop(0, n_full, full_step, 0)
    sel_step(n_full, True)
    for k in range(N_KV_A):
        acc = acc_scr[k]
        o_sel = acc[:HEAD_DIM_A] * normalise(acc)
        for g in range(GROUP_A):
            h = k * GROUP_A + g
            lanes = slice(g * Q_BLOCK, (g + 1) * Q_BLOCK)
            rows = slice(h * HEAD_DIM_A, (h + 1) * HEAD_DIM_A)
            ot_scr[rows, :] = ot_scr[rows, :] + gates_t[N_HEADS_A + h:N_HEADS_A + h + 1, :] * o_sel[:, lanes]

    for c in range(D_A // LANES):
        cols = slice(c * LANES, (c + 1) * LANES)
        o_ref[:, cols] = (ot_scr[cols, :].T * _silu(ga_ref[:, cols])).astype(o_ref.dtype)


def _nsa_prompt(z, ck, cvt, sk, svt, wk, wvt):
    B, T, _ = z.shape
    n_cmp = ck.shape[1]
    et = _block_expand(LANES, T).T
    ovt = _overlap(n_cmp, LANES).T
    once = pl.Buffered(1)
    gq = GROUP_A * Q_BLOCK
    vt_w = N_KV_A * V_ROWS

    def whole(shape):
        return pl.BlockSpec((None,) + shape, lambda b, i: (b, 0, 0), pipeline_mode=once)

    return pl.pallas_call(
        _nsa_prompt_kernel,
        out_shape=jax.ShapeDtypeStruct((B, T, D_A), BF16),
        grid=(B, T // Q_BLOCK),
        in_specs=[pl.BlockSpec((None, Q_BLOCK, D_A), lambda b, i: (b, i, COL_Q // D_A)),
                  pl.BlockSpec((None, Q_BLOCK, D_A), lambda b, i: (b, i, COL_GA // D_A)),
                  pl.BlockSpec((None, Q_BLOCK, LANES), lambda b, i: (b, i, COL_G // LANES)),
                  whole((n_cmp, KV_W)), whole((vt_w, n_cmp)),
                  whole((T, KV_W)), whole((vt_w, T)), whole((T, KV_W)), whole((vt_w, T)),
                  pl.BlockSpec((T, LANES), lambda b, i: (0, 0), pipeline_mode=once),
                  pl.BlockSpec((LANES, n_cmp), lambda b, i: (0, 0), pipeline_mode=once)],
        out_specs=pl.BlockSpec((None, Q_BLOCK, D_A), lambda b, i: (b, i, 0)),
        scratch_shapes=[pltpu.VMEM((D_A, Q_BLOCK), BF16), pltpu.VMEM((D_A, Q_BLOCK), F32),
                        pltpu.VMEM((N_KV_A, 2 * LANES, gq), BF16),
                        pltpu.VMEM((N_KV_A, 1, gq), F32), pltpu.VMEM((N_KV_A, V_ROWS, gq), F32)],
        compiler_params=_cparams(("parallel", "arbitrary")),
        name="nsa_prompt",
    )(z, z, z, ck, cvt, sk, svt, wk, wvt, et, ovt)


def _nsa_sample_kernel(tab_ref, *refs, past, n_steps, t_new):
    pages = refs[:PAGES_PER_STEP]
    (q_ref, ckt_ref, cv_ref, new_ref, wkv_ref, g_ref, ga_ref, e_ref, ov_ref, o_ref, hist) = refs[PAGES_PER_STEP:]
    s = pl.program_id(1)
    for r in range(PAGES_PER_STEP):
        col = pl.multiple_of((s * PAGES_PER_STEP + r) * PAGE_SIZE, PAGE_SIZE)
        hist[:, pl.ds(col, PAGE_SIZE)] = pages[r][...]

    @pl.when(s == 0)
    def _():
        hist[:, past:] = new_ref[...]

    @pl.when(s == n_steps - 1)
    def _():
        n_rows = N_HEADS_A * t_new
        n_keys = hist.shape[1]
        n_cmp = ckt_ref.shape[1]
        n_blk = ov_ref.shape[1]
        n_win = wkv_ref.shape[0]
        q = q_ref[...]
        row = lax.broadcasted_iota(jnp.int32, (n_rows, 1), 0)
        pos_q = past + row % t_new
        row_k = (row // t_new) % N_KV_A

        def diag(o_all):
            out = jnp.zeros((n_rows, HEAD_DIM_A), F32)
            for k in range(N_KV_A):
                out = out + jnp.where(row_k == k, o_all[:, k * HEAD_DIM_A:(k + 1) * HEAD_DIM_A], 0.0)
            return out

        cmp_end = lax.broadcasted_iota(jnp.int32, (1, n_cmp), 1) * CMP_STRIDE + (CMP_BLOCK - 1)
        s_c = jnp.dot(q, ckt_ref[...], preferred_element_type=F32)
        p = _masked_softmax(s_c, cmp_end <= pos_q).astype(BF16)
        o_cmp = diag(jnp.dot(p, cv_ref[...], preferred_element_type=F32))
        pj = jnp.dot(p, ov_ref[...], preferred_element_type=F32)
        rows_kq = N_KV_A * t_new
        p_slc = pj[0:rows_kq]
        for g in range(1, GROUP_A):
            p_slc = p_slc + pj[g * rows_kq:(g + 1) * rows_kq]

        j = lax.broadcasted_iota(jnp.int32, (rows_kq, n_blk), 1)
        pos_r = past + lax.broadcasted_iota(jnp.int32, (rows_kq, n_blk), 0) % t_new
        cur = pos_r // SEL_BLOCK
        valid = j * SEL_BLOCK <= pos_r
        forced = (j == 0) | (j == cur) | (j == cur - 1)
        score = jnp.where(valid, jnp.where(forced, FORCE_SCORE, p_slc), -jnp.inf)
        sel = _top_blocks(score, j, 1)
        sel = jnp.concatenate([sel] * GROUP_A, axis=0).astype(BF16)

        picked = jnp.dot(sel, e_ref[...], preferred_element_type=F32)
        s_s = jnp.dot(q, hist[0:KV_W, :].astype(BF16), preferred_element_type=F32)
        pos_k = lax.broadcasted_iota(jnp.int32, (1, n_keys), 1)
        p = _masked_softmax(s_s, (picked > 0.5) & (pos_k <= pos_q)).astype(BF16)
        o_sel = diag(lax.dot_general(p, hist[KV_W:, :].astype(BF16), (((1,), (1,)), ((), ())),
                                     preferred_element_type=F32))

        wk = wkv_ref[:, 0:KV_W].astype(BF16)
        s_w = lax.dot_general(q, wk, (((1,), (1,)), ((), ())), preferred_element_type=F32)
        pos_w = past - WINDOW + lax.broadcasted_iota(jnp.int32, (1, n_win), 1)
        p = _masked_softmax(s_w, (pos_w <= pos_q) & (pos_q - pos_w < WINDOW)).astype(BF16)
        o_win = diag(jnp.dot(p, wkv_ref[:, KV_W:].astype(BF16), preferred_element_type=F32))

        gates = jax.nn.sigmoid(g_ref[...])
        o = gates[:, 0:1] * o_cmp + gates[:, 1:2] * o_sel + gates[:, 2:3] * o_win
        o_ref[...] = o * _silu(ga_ref[...])


def _nsa_sample(sel_view, table, qbs, ckt, cv, new_sel, wkv, g_rows, ga_rows, *, past, t_new):
    NB, n_pages = table.shape
    n_steps = n_pages // PAGES_PER_STEP
    n_keys = past + PAGE_SIZE
    n_cmp = ckt.shape[2]
    n_blk = 2 * LANES
    n_rows = qbs.shape[1]
    e = _block_expand(n_blk, n_keys)
    ov = _overlap(n_cmp, n_blk)

    def page_spec(r):
        return pl.BlockSpec((None, KV_BRANCH, PAGE_SIZE), lambda b, s, tab: (tab[b, s * PAGES_PER_STEP + r], 0, 0))

    def per_b(arr):
        return pl.BlockSpec((None,) + arr.shape[1:], lambda b, s, tab: (b, 0, 0))

    const2 = lambda b, s, tab: (0, 0)
    return pl.pallas_call(
        functools.partial(_nsa_sample_kernel, past=past, n_steps=n_steps, t_new=t_new),
        out_shape=jax.ShapeDtypeStruct((NB, n_rows, HEAD_DIM_A), F32),
        grid_spec=pltpu.PrefetchScalarGridSpec(
            num_scalar_prefetch=1,
            grid=(NB, n_steps),
            in_specs=[page_spec(r) for r in range(PAGES_PER_STEP)] + [
                per_b(qbs), per_b(ckt), per_b(cv), per_b(new_sel), per_b(wkv), per_b(g_rows), per_b(ga_rows),
                pl.BlockSpec(e.shape, const2, pipeline_mode=pl.Buffered(1)),
                pl.BlockSpec(ov.shape, const2, pipeline_mode=pl.Buffered(1))],
            out_specs=pl.BlockSpec((None, n_rows, HEAD_DIM_A), lambda b, s, tab: (b, 0, 0)),
            scratch_shapes=[pltpu.VMEM((KV_BRANCH, n_keys), F32)]),
        compiler_params=_cparams(("parallel", "arbitrary")),
        name="nsa_sample",
    )(table, *([sel_view] * PAGES_PER_STEP), qbs, ckt, cv, new_sel, wkv, g_rows, ga_rows, e, ov)


def _permute_w_in_a(w):
    c_kv, c_g, c_ga, c_glu, c_gb = 1024, 2560, 2608, 3632, 5680
    cols = [w[:, :c_kv], w[:, c_ga:c_glu], w[:, c_glu:c_gb], w[:, c_gb:], w[:, c_kv:c_g], w[:, c_g:c_ga]]
    n = sum(c.shape[1] for c in cols)
    return jnp.concatenate(cols + [jnp.zeros((w.shape[0], EVEN_N - n), w.dtype)], axis=1).astype(BF16)


def _pad_rows(x, front, total):
    return jnp.pad(x, ((0, 0), (front, total - front - x.shape[1]), (0, 0)))


def _even_prompt(z, cmpw, conv_args, *, tq):
    B, T, _ = z.shape
    kv = z[:, :, COL_KV:COL_G]
    kv_cmp, kv_sel, kv_win = (kv[:, :, br * KV_BRANCH:(br + 1) * KV_BRANCH] for br in range(3))
    n_pages = T // PAGE_SIZE
    table = jnp.arange(B * n_pages, dtype=jnp.int32).reshape(B, n_pages)
    ckv = _compress(kv_cmp.reshape(B * n_pages, PAGE_SIZE, KV_BRANCH), table, *cmpw, transposed=False)

    def split_kv(x):
        return x[:, :, :KV_W].astype(BF16), _values_t(x[:, :, KV_W:])

    o_a = _nsa_prompt(z, *split_kv(ckv), *_kv_split(z, COL_KV + KV_BRANCH, tt=tq),
                      *_kv_split(z, COL_KV + 2 * KV_BRANCH, tt=tq))
    u = _glu(z, tt=tq)
    halo_map = lambda bb, i: (bb, jnp.maximum(i * (tq // CONV_HALO) - 1, 0), 0)
    o_b = _conv(u, halo_map, True, u, z, COL_GB // D_MODEL, *conv_args, tt=tq)
    kv_shape = (B, T, 2, N_KV_A, HEAD_DIM_A)
    w_keep = min(WINDOW, T)
    outs = (kv_cmp.reshape(kv_shape), kv_sel.reshape(kv_shape),
            kv_win[:, T - w_keep:].reshape(B, w_keep, 2, N_KV_A, HEAD_DIM_A), u[:, T - (CONV_WIDTH - 1):])
    return o_a, o_b, outs


def _even_sample(z, cmpw, conv_args, cmp_view, sel_view, table, win_buf, conv_hist, *, past):
    NB, t_new, _ = z.shape
    kv = z[:, :, COL_KV:COL_G]
    kv_cmp, kv_sel, kv_win = (kv[:, :, br * KV_BRANCH:(br + 1) * KV_BRANCH] for br in range(3))
    ckv = _compress(cmp_view, table, *cmpw, transposed=True)
    q = z[:, :, COL_Q:COL_Q + D_A].reshape(NB, t_new, N_KV_A, GROUP_A, HEAD_DIM_A) * (HEAD_DIM_A ** -0.5)
    qbs = jnp.einsum('bqkgd,kK->bgkqKd', q, jnp.eye(N_KV_A, dtype=F32)).reshape(NB, N_HEADS_A * t_new, KV_W)
    g_rows = z[:, :, COL_G:COL_G + 3 * N_HEADS_A].reshape(NB, t_new, 3, N_KV_A, GROUP_A)
    g_rows = jnp.transpose(g_rows, (0, 4, 3, 1, 2)).reshape(NB, N_HEADS_A * t_new, 3)
    g_rows = jnp.pad(g_rows, ((0, 0), (0, 0), (0, LANES - 3)))
    ga_rows = z[:, :, COL_GA:COL_GA + D_A].reshape(NB, t_new, N_KV_A, GROUP_A, HEAD_DIM_A)
    ga_rows = jnp.transpose(ga_rows, (0, 3, 2, 1, 4)).reshape(NB, N_HEADS_A * t_new, HEAD_DIM_A)
    wkv = jnp.concatenate([win_buf.reshape(NB, -1, KV_BRANCH), kv_win], axis=1)
    wb = win_buf.shape[1]
    o = _nsa_sample(sel_view, table, qbs.astype(BF16),
                    jnp.swapaxes(ckv[:, :, :KV_W], 1, 2).astype(BF16), ckv[:, :, KV_W:].astype(BF16),
                    jnp.swapaxes(_pad_rows(kv_sel, 0, PAGE_SIZE), 1, 2), _pad_rows(wkv, 0, WIN_KEYS), g_rows, ga_rows,
                    past=past, t_new=t_new)
    o_a = jnp.transpose(o.reshape(NB, GROUP_A, N_KV_A, t_new, HEAD_DIM_A), (0, 3, 2, 1, 4))
    o_a = o_a.reshape(NB * t_new, D_A).astype(BF16)
    u = _glu(z.reshape(1, NB * t_new, EVEN_N), tt=NB * t_new).reshape(NB, t_new, D_MODEL)
    halo = _pad_rows(conv_hist, CONV_HALO - (CONV_WIDTH - 1), CONV_HALO)
    gate_b = _pad_rows(z[:, :, COL_GB:COL_GB + D_MODEL], 0, SUBLANES)
    o_b = _conv(halo, lambda bb, i: (bb, 0, 0), False, _pad_rows(u, 0, SUBLANES), gate_b, 0, *conv_args, tt=SUBLANES)
    o_b = o_b[:, :t_new].reshape(NB * t_new, D_MODEL)
    kv_shape = (NB, t_new, 2, N_KV_A, HEAD_DIM_A)
    outs = (kv_cmp.reshape(kv_shape), kv_sel.reshape(kv_shape),
            wkv[:, t_new:].reshape(NB, wb, 2, N_KV_A, HEAD_DIM_A),
            jnp.concatenate([conv_hist, u], axis=1)[:, -(CONV_WIDTH - 1):])
    return o_a, o_b, outs


def kernel(x_prompt, x_sample, mem_prompt, cache_cmp_kv, cache_sel_kv, cache_win_kv, state_conv, state_pool,
           cache_mem_kv, page_table, norm_mix, norm_mem, norm_out, w_in_a, w_out_a, cmp_pe, cmp_w1, cmp_w2,
           conv_w, conv_b, conv_ln_g, conv_ln_b, w_in_c, pool_w, pool_scale, w_out_c, w_mem_q, w_mem_kv, w_mem_o):
    B, T, D = x_prompt.shape
    NB, t_new, _ = x_sample.shape
    depth = norm_mix.shape[0]
    n_pool = cache_cmp_kv.shape[1]
    past = page_table.shape[1] * PAGE_SIZE
    mem_len = mem_prompt.shape[1]
    tm_p = 1024
    tq = 256
    ms = NB * t_new
    xp = x_prompt.reshape(B * T, D)
    xs = x_sample.reshape(ms, D)
    p_cmp, p_sel, p_win, p_conv, p_pool, p_mem = [], [], [], [], [], []
    s_cmp, s_sel, s_win, s_conv, s_pool = [], [], [], [], []

    def pages_t(cache):
        return jnp.transpose(cache, (0, 1, 3, 4, 5, 2)).reshape(-1, KV_BRANCH, PAGE_SIZE)

    cmp_view = pages_t(cache_cmp_kv)
    sel_view = pages_t(cache_sel_kv)

    for layer in range(depth):
        if layer % 2 == 0:
            a = layer // 2
            w_in = _permute_w_in_a(w_in_a[a])
            w_out = w_out_a[a].astype(BF16)
            cmpw = _compress_weights(cmp_pe[a], cmp_w1[a], cmp_w2[a])
            conv_args = (conv_w[a], conv_b[a], conv_ln_g[a], conv_ln_b[a])
            zp = _mm([xp], w_in, gain=norm_mix[layer], tm=tm_p, tn=EVEN_TN).reshape(B, T, EVEN_N)
            o_a, o_b, outs = _even_prompt(zp, cmpw, conv_args, tq=tq)
            xp = _mm([o_a.reshape(B * T, D_A), o_b.reshape(B * T, D)], w_out, res=xp, tm=tm_p, tn=D)
            for lst, val in zip((p_cmp, p_sel, p_win, p_conv), outs):
                lst.append(val)
            zs = _mm([xs], w_in, gain=norm_mix[layer], tm=ms, tn=EVEN_TN).reshape(NB, t_new, EVEN_N)
            table = page_table + a * n_pool
            o_a, o_b, outs = _even_sample(zs, cmpw, conv_args, cmp_view, sel_view, table, cache_win_kv[a],
                                          state_conv[a], past=past)
            xs = _mm([o_a, o_b], w_out, res=xs, tm=ms, tn=D)
            for lst, val in zip((s_cmp, s_sel, s_win, s_conv), outs):
                lst.append(val)
        else:
            c = layer // 2
            w_in = w_in_c[c].astype(BF16)
            w_out = w_out_c[c].astype(BF16)
            zp = _mm([xp], w_in, gain=norm_mix[layer], tm=tm_p, tn=D).reshape(B, T, 2 * D)
            halo_map = lambda bb, i: (bb, jnp.maximum(i * (tq // POOL_HALO) - 1, 0), 0)
            m = _pool(zp, halo_map, True, zp, pool_w[c], pool_scale[c], tt=tq, pos0=0)
            xp = _mm([m.reshape(B * T, D)], w_out, res=xp, tm=tm_p, tn=D)
            p_pool.append(zp[:, T - POOL_HIST:, :D])
            zs = _mm([xs], w_in, gain=norm_mix[layer], tm=ms, tn=D).reshape(NB, t_new, 2 * D)
            hist = state_pool[c]
            m = _pool(_pad_rows(hist, POOL_HALO - POOL_HIST, POOL_HALO), lambda bb, i: (bb, 0, 0), False,
                      _pad_rows(zs, 0, SUBLANES), pool_w[c], pool_scale[c], tt=SUBLANES, pos0=past)
            xs = _mm([m[:, :t_new].reshape(ms, D)], w_out, res=xs, tm=ms, tn=D)
            s_pool.append(jnp.concatenate([hist, zs[:, :, :D]], axis=1)[:, -POOL_HIST:])

        wq = w_mem_q[layer].astype(BF16)
        wo = w_mem_o[layer].astype(BF16)
        mkv = _mm([mem_prompt.reshape(B * mem_len, D)], w_mem_kv[layer].astype(BF16), tm=B * mem_len, tn=D)
        mkv = mkv.reshape(B, mem_len, 2 * D)
        p_mem.append(mkv.reshape(B, mem_len, 2, N_HEADS_M, HEAD_DIM_M))
        q = _mm([xp], wq, gain=norm_mem[layer], tm=tm_p, tn=D, out_dtype=BF16).reshape(B, T, D)
        o = _mem_attn(q, jnp.swapaxes(mkv[:, :, :D], 1, 2).astype(BF16), mkv[:, :, D:].astype(BF16), tm=tq)
        xp = _mm([o.reshape(B * T, D)], wo, res=xp, tm=tm_p, tn=D)
        ckv = cache_mem_kv[layer].reshape(NB, mem_len, 2 * D)
        q = _mm([xs], wq, gain=norm_mem[layer], tm=ms, tn=D, out_dtype=BF16).reshape(NB, t_new, D)
        o = _mem_attn(_pad_rows(q, 0, SUBLANES), jnp.swapaxes(ckv[:, :, :D], 1, 2).astype(BF16),
                      ckv[:, :, D:].astype(BF16), tm=SUBLANES)
        xs = _mm([o[:, :t_new].reshape(ms, D)], wo, res=xs, tm=ms, tn=D)

    y_prompt = _rmsnorm(xp, norm_out, tm=tm_p).reshape(B, T, D)
    y_sample = _rmsnorm(xs, norm_out, tm=ms).reshape(NB, t_new, D)
    return (y_prompt, y_sample,
            jnp.stack(p_cmp), jnp.stack(p_sel), jnp.stack(p_win), jnp.stack(p_conv), jnp.stack(p_pool),
            jnp.stack(p_mem),
            jnp.stack(s_cmp), jnp.stack(s_sel), jnp.stack(s_win), jnp.stack(s_conv), jnp.stack(s_pool))
```

```python
import functools

import jax
import jax.numpy as jnp
from jax import lax
from jax.experimental import pallas as pl
from jax.experimental.pallas import tpu as pltpu

F32 = jnp.float32
BF16 = jnp.bfloat16

LANES = 128
SUBLANES = 8
VMEM_LIMIT_BYTES = 56 * 1024 * 1024

D_MODEL = 1024
N_HEADS_A = 16
HEAD_DIM_A = 64
N_KV_A = 4
GROUP_A = N_HEADS_A // N_KV_A
D_A = N_HEADS_A * HEAD_DIM_A
KV_W = N_KV_A * HEAD_DIM_A
KV_BRANCH = 2 * KV_W
CMP_STRIDE = 16
CMP_BLOCK = 32
CMP_HIDDEN = 128
SEL_BLOCK = 64
N_SEL = 16
WINDOW = 512
Q_BLOCK = 128
FORCE_SCORE = 1000.0
CONV_WIDTH = 31
CONV_HALO = 32
POOL_WINDOWS = (2, 4, 8, 16)
POOL_GROUP = D_MODEL // len(POOL_WINDOWS)
POOL_HIST = 15
POOL_HALO = 16
PAGE_SIZE = 128
N_HEADS_M = 4
HEAD_DIM_M = 256
NORM_EPS = 1e-6
NEG = -1e30

COL_Q, COL_GA, COL_VAL, COL_GL, COL_GB = 0, 1024, 2048, 3072, 4096
COL_KV = 5120
COL_G = COL_KV + 3 * KV_BRANCH
EVEN_N = 6912
EVEN_TN = 768
PAGES_PER_STEP = 8


def _cparams(sem):
    return pltpu.CompilerParams(dimension_semantics=sem, vmem_limit_bytes=VMEM_LIMIT_BYTES)


def _silu(x):
    return x * jax.nn.sigmoid(x)


def _mm_kernel(*refs, n_parts, has_gain, has_res):
    parts = refs[:n_parts]
    pos = n_parts
    gain_ref = refs[pos] if has_gain else None
    pos += int(has_gain)
    w_ref = refs[pos]
    pos += 1
    res_ref = refs[pos] if has_res else None
    pos += int(has_res)
    o_ref, lhs = refs[pos], refs[pos + 1]

    @pl.when(pl.program_id(1) == 0)
    def _():
        off = 0
        for p in parts:
            x = p[...].astype(F32)
            if has_gain:
                x = x * lax.rsqrt(jnp.mean(x * x, axis=-1, keepdims=True) + NORM_EPS) * gain_ref[...]
            lhs[:, off:off + p.shape[1]] = x.astype(BF16)
            off += p.shape[1]

    acc = jnp.dot(lhs[...], w_ref[...], preferred_element_type=F32)
    if has_res:
        acc = acc + res_ref[...]
    o_ref[...] = acc.astype(o_ref.dtype)


def _mm(parts, w, *, gain=None, res=None, tm, tn, out_dtype=F32):
    M = parts[0].shape[0]
    K, N = w.shape
    assert sum(p.shape[1] for p in parts) == K and M % tm == 0 and N % tn == 0
    in_specs = [pl.BlockSpec((tm, p.shape[1]), lambda i, j: (i, 0)) for p in parts]
    args = list(parts)
    if gain is not None:
        in_specs.append(pl.BlockSpec((1, K), lambda i, j: (0, 0)))
        args.append(gain.reshape(1, K).astype(F32))
    in_specs.append(pl.BlockSpec((K, tn), lambda i, j: (0, j)))
    args.append(w)
    if res is not None:
        in_specs.append(pl.BlockSpec((tm, tn), lambda i, j: (i, j)))
        args.append(res)
    return pl.pallas_call(
        functools.partial(_mm_kernel, n_parts=len(parts), has_gain=gain is not None, has_res=res is not None),
        out_shape=jax.ShapeDtypeStruct((M, N), out_dtype),
        grid=(M // tm, N // tn),
        in_specs=in_specs,
        out_specs=pl.BlockSpec((tm, tn), lambda i, j: (i, j)),
        scratch_shapes=[pltpu.VMEM((tm, K), BF16)],
        compiler_params=_cparams(("parallel", "arbitrary")),
        name="mm",
    )(*args)


def _rmsnorm_kernel(x_ref, g_ref, o_ref):
    x = x_ref[...]
    o_ref[...] = x * lax.rsqrt(jnp.mean(x * x, axis=-1, keepdims=True) + NORM_EPS) * g_ref[...]


def _rmsnorm(x, g, *, tm):
    M, D = x.shape
    return pl.pallas_call(
        _rmsnorm_kernel,
        out_shape=jax.ShapeDtypeStruct((M, D), F32),
        grid=(M // tm,),
        in_specs=[pl.BlockSpec((tm, D), lambda i: (i, 0)), pl.BlockSpec((1, D), lambda i: (0, 0))],
        out_specs=pl.BlockSpec((tm, D), lambda i: (i, 0)),
        compiler_params=_cparams(("parallel",)),
        name="rmsnorm",
    )(x, g.reshape(1, D))


def _glu_kernel(val_ref, gl_ref, o_ref):
    o_ref[...] = val_ref[...] * jax.nn.sigmoid(gl_ref[...])


def _glu(z, *, tt):
    B, T, _ = z.shape
    blk = (None, tt, D_MODEL)
    return pl.pallas_call(
        _glu_kernel,
        out_shape=jax.ShapeDtypeStruct((B, T, D_MODEL), F32),
        grid=(B, T // tt),
        in_specs=[pl.BlockSpec(blk, lambda b, i: (b, i, COL_VAL // D_MODEL)),
                  pl.BlockSpec(blk, lambda b, i: (b, i, COL_GL // D_MODEL))],
        out_specs=pl.BlockSpec(blk, lambda b, i: (b, i, 0)),
        compiler_params=_cparams(("parallel", "parallel")),
        name="glu",
    )(z, z)


CONV_ROWS = 32


def _conv_kernel(halo_ref, cur_ref, gate_ref, w_ref, b_ref, lg_ref, lb_ref, o_ref, buf, sh, y_scr, *, tt, zero_first_halo):
    halo = halo_ref[...]
    if zero_first_halo:
        halo = jnp.where(pl.program_id(1) == 0, 0.0, halo)
    buf[0:CONV_HALO, :] = halo
    buf[CONV_HALO:CONV_HALO + tt, :] = cur_ref[...]
    buf[CONV_HALO + tt:, :] = jnp.zeros((SUBLANES, D_MODEL), F32)
    first = CONV_HALO - (CONV_WIDTH - 1)
    n_sh = sh.shape[1]
    for r in range(SUBLANES):
        sh[r] = buf[first + r:first + r + n_sh, :]
    rc = min(CONV_ROWS, tt)
    for r0 in range(0, tt, rc):
        acc = jnp.broadcast_to(b_ref[...], (rc, D_MODEL))
        for j in range(CONV_WIDTH):
            a0 = r0 + (j // SUBLANES) * SUBLANES
            acc = acc + w_ref[j:j + 1, :] * sh[j % SUBLANES, a0:a0 + rc, :]
        y_scr[r0:r0 + rc, :] = acc
    y = y_scr[...]
    xc = y - jnp.mean(y, axis=-1, keepdims=True)
    yn = xc * lax.rsqrt(jnp.mean(xc * xc, axis=-1, keepdims=True) + NORM_EPS) * lg_ref[...] + lb_ref[...]
    o_ref[...] = (_silu(yn) * _silu(gate_ref[...])).astype(o_ref.dtype)


def _conv(halo_arr, halo_map, zero_first_halo, u, gate_arr, gate_col, w, b, lg, lb, *, tt):
    B, T, _ = u.shape
    w32 = jnp.concatenate([w, jnp.zeros((CONV_HALO - CONV_WIDTH + 1, D_MODEL), F32)], axis=0)[:CONV_HALO]
    vec = pl.BlockSpec((1, D_MODEL), lambda bb, i: (0, 0))
    return pl.pallas_call(
        functools.partial(_conv_kernel, tt=tt, zero_first_halo=zero_first_halo),
        out_shape=jax.ShapeDtypeStruct((B, T, D_MODEL), BF16),
        grid=(B, T // tt),
        in_specs=[pl.BlockSpec((None, CONV_HALO, D_MODEL), halo_map),
                  pl.BlockSpec((None, tt, D_MODEL), lambda bb, i: (bb, i, 0)),
                  pl.BlockSpec((None, tt, D_MODEL), lambda bb, i: (bb, i, gate_col)),
                  pl.BlockSpec((CONV_HALO, D_MODEL), lambda bb, i: (0, 0)),
                  vec, vec, vec],
        out_specs=pl.BlockSpec((None, tt, D_MODEL), lambda bb, i: (bb, i, 0)),
        scratch_shapes=[pltpu.VMEM((CONV_HALO + tt + SUBLANES, D_MODEL), F32),
                        pltpu.VMEM((SUBLANES, tt + CONV_HALO - SUBLANES, D_MODEL), F32),
                        pltpu.VMEM((tt, D_MODEL), F32)],
        compiler_params=_cparams(("parallel", "parallel")),
        name="conv",
    )(halo_arr, u, gate_arr, w32, b.reshape(1, -1), lg.reshape(1, -1), lb.reshape(1, -1))


def _pool_kernel(halo_ref, cur_ref, gate_ref, pw_ref, ps_ref, o_ref, buf, *, tt, zero_first_halo, pos0):
    i = pl.program_id(1)
    halo = halo_ref[...]
    if zero_first_halo:
        halo = jnp.where(i == 0, 0.0, halo)
    buf[0:POOL_HALO, :] = halo
    buf[POOL_HALO:POOL_HALO + tt, :] = cur_ref[...]
    pos = pos0 + i * tt + lax.broadcasted_iota(jnp.int32, (tt, 1), 0)
    for g, w in enumerate(POOL_WINDOWS):
        c0, c1 = g * POOL_GROUP, (g + 1) * POOL_GROUP
        s = buf[POOL_HALO:POOL_HALO + tt, c0:c1]
        for j in range(1, w):
            s = s + buf[POOL_HALO - j:POOL_HALO - j + tt, c0:c1]
        cnt = jnp.minimum(w, pos + 1).astype(F32)
        d = s / cnt - buf[POOL_HALO:POOL_HALO + tt, c0:c1]
        m = jnp.dot(d.astype(BF16), pw_ref[g], preferred_element_type=F32)
        o_ref[:, c0:c1] = (m * ps_ref[:, c0:c1] * _silu(gate_ref[:, c0:c1])).astype(o_ref.dtype)


def _pool(halo_arr, halo_map, zero_first_halo, z, pool_w, pool_scale, *, tt, pos0):
    B, T, _ = z.shape
    return pl.pallas_call(
        functools.partial(_pool_kernel, tt=tt, zero_first_halo=zero_first_halo, pos0=pos0),
        out_shape=jax.ShapeDtypeStruct((B, T, D_MODEL), BF16),
        grid=(B, T // tt),
        in_specs=[pl.BlockSpec((None, POOL_HALO, D_MODEL), halo_map),
                  pl.BlockSpec((None, tt, D_MODEL), lambda bb, i: (bb, i, 0)),
                  pl.BlockSpec((None, tt, D_MODEL), lambda bb, i: (bb, i, 1)),
                  pl.BlockSpec((len(POOL_WINDOWS), POOL_GROUP, POOL_GROUP), lambda bb, i: (0, 0, 0)),
                  pl.BlockSpec((1, D_MODEL), lambda bb, i: (0, 0))],
        out_specs=pl.BlockSpec((None, tt, D_MODEL), lambda bb, i: (bb, i, 0)),
        scratch_shapes=[pltpu.VMEM((POOL_HALO + tt, D_MODEL), F32)],
        compiler_params=_cparams(("parallel", "parallel")),
        name="pool",
    )(halo_arr, z, z, pool_w.astype(BF16), pool_scale.reshape(1, -1))


def _mem_attn_kernel(q_ref, kt_ref, v_ref, o_ref):
    for h in range(N_HEADS_M):
        c0, c1 = h * HEAD_DIM_M, (h + 1) * HEAD_DIM_M
        q = q_ref[:, c0:c1] * (HEAD_DIM_M ** -0.5)
        s = jnp.dot(q, kt_ref[c0:c1, :], preferred_element_type=F32)
        e = jnp.exp(s - jnp.max(s, axis=-1, keepdims=True))
        p = e / jnp.sum(e, axis=-1, keepdims=True)
        o = jnp.dot(p.astype(BF16), v_ref[:, c0:c1], preferred_element_type=F32)
        o_ref[:, c0:c1] = o.astype(o_ref.dtype)


def _mem_attn(q, kt, v, *, tm):
    NB, Tq, D = q.shape
    M = v.shape[1]
    return pl.pallas_call(
        _mem_attn_kernel,
        out_shape=jax.ShapeDtypeStruct((NB, Tq, D), BF16),
        grid=(NB, Tq // tm),
        in_specs=[pl.BlockSpec((None, tm, D), lambda b, i: (b, i, 0)),
                  pl.BlockSpec((None, D, M), lambda b, i: (b, 0, 0)),
                  pl.BlockSpec((None, M, D), lambda b, i: (b, 0, 0))],
        out_specs=pl.BlockSpec((None, tm, D), lambda b, i: (b, i, 0)),
        compiler_params=_cparams(("parallel", "parallel")),
        name="mem_attn",
    )(q, kt, v)


def _compress_weights(pe, w1, w2):
    eye2 = jnp.eye(2, dtype=F32)

    def bd(wc):
        return jnp.einsum('ldh,kK->lkdKh', wc, eye2).reshape(CMP_STRIDE * LANES, 2 * CMP_HIDDEN)

    w1bd = jnp.stack([jnp.concatenate([bd(w1[c, :CMP_STRIDE]), bd(w1[c, CMP_STRIDE:])], axis=1)
                      for c in range(2)]).astype(BF16)
    w2bd = jnp.stack([jnp.einsum('hd,kK->khKd', w2[c], eye2).reshape(2 * CMP_HIDDEN, LANES)
                      for c in range(2)]).astype(BF16)
    pe_rows = jnp.zeros((2, 2 * SUBLANES, CMP_BLOCK * HEAD_DIM_A), F32).at[:, 0].set(pe.reshape(2, -1))
    bias = jnp.stack([_mm([pe_rows[c]], w1[c].reshape(CMP_BLOCK * HEAD_DIM_A, CMP_HIDDEN).astype(BF16),
                          tm=2 * SUBLANES, tn=CMP_HIDDEN)[0] for c in range(2)])
    bias2 = jnp.concatenate([bias, bias], axis=1).reshape(2, 1, 2 * CMP_HIDDEN)
    return w1bd, w2bd, bias2


def _compress_kernel(tab_ref, *refs, n_pos, n_steps, transposed):
    pages = refs[:PAGES_PER_STEP]
    w1_ref, bias_ref, w2_ref, o_ref, hist, sec = refs[PAGES_PER_STEP:]
    s = pl.program_id(1)
    n_grp = KV_BRANCH // LANES
    per_page = PAGE_SIZE // CMP_STRIDE
    n_rows = n_pos // CMP_STRIDE
    out_row = lax.broadcasted_iota(jnp.int32, (PAGE_SIZE, PAGE_SIZE), 0)
    in_pos = lax.broadcasted_iota(jnp.int32, (PAGE_SIZE, PAGE_SIZE), 1)
    perm = (in_pos == (out_row % per_page) * CMP_STRIDE + out_row // per_page).astype(BF16)
    for r in range(PAGES_PER_STEP):
        row = pl.multiple_of((s * PAGES_PER_STEP + r) * per_page, per_page)
        page = pages[r][...].astype(BF16)
        if transposed:
            x = lax.dot_general(perm, page, (((1,), (1,)), ((), ())), preferred_element_type=F32)
        else:
            x = jnp.dot(perm, page, preferred_element_type=F32)
        for grp in range(n_grp):
            for l in range(CMP_STRIDE):
                hist[grp, l, pl.ds(row, per_page), :] = x[l * per_page:(l + 1) * per_page,
                                                          grp * LANES:(grp + 1) * LANES]

    @pl.when(s == 0)
    def _():
        hist[:, :, n_rows:, :] = jnp.zeros((n_grp, CMP_STRIDE, hist.shape[2] - n_rows, LANES), F32)

    @pl.when(s == n_steps - 1)
    def _():
        for grp in range(n_grp):
            c = grp // 2
            xg = jnp.concatenate([hist[grp, l] for l in range(CMP_STRIDE)], axis=1).astype(BF16)
            fs = jnp.dot(xg, w1_ref[c], preferred_element_type=F32)
            sec[...] = fs[:, 2 * CMP_HIDDEN:]
            hid = _silu(fs[0:n_rows, :2 * CMP_HIDDEN] + sec[1:n_rows + 1, :] + bias_ref[c])
            o_ref[:, grp * LANES:(grp + 1) * LANES] = jnp.dot(hid.astype(BF16), w2_ref[c],
                                                              preferred_element_type=F32)


def _compress(pages, table, w1bd, w2bd, bias2, *, transposed):
    NB, n_pages = table.shape
    n_steps = n_pages // PAGES_PER_STEP
    n_pos = n_pages * PAGE_SIZE
    n_rows = n_pos // CMP_STRIDE

    def page_spec(r):
        return pl.BlockSpec((None,) + pages.shape[1:], lambda b, s, tab: (tab[b, s * PAGES_PER_STEP + r], 0, 0))

    const3 = lambda b, s, tab: (0, 0, 0)
    return pl.pallas_call(
        functools.partial(_compress_kernel, n_pos=n_pos, n_steps=n_steps, transposed=transposed),
        out_shape=jax.ShapeDtypeStruct((NB, n_rows, KV_BRANCH), F32),
        grid_spec=pltpu.PrefetchScalarGridSpec(
            num_scalar_prefetch=1,
            grid=(NB, n_steps),
            in_specs=[page_spec(r) for r in range(PAGES_PER_STEP)] + [
                pl.BlockSpec(w1bd.shape, const3), pl.BlockSpec(bias2.shape, const3), pl.BlockSpec(w2bd.shape, const3)],
            out_specs=pl.BlockSpec((None, n_rows, KV_BRANCH), lambda b, s, tab: (b, 0, 0)),
            scratch_shapes=[pltpu.VMEM((KV_BRANCH // LANES, CMP_STRIDE, n_rows + SUBLANES, LANES), F32),
                            pltpu.VMEM((n_rows + SUBLANES, 2 * CMP_HIDDEN), F32)]),
        compiler_params=_cparams(("parallel", "arbitrary")),
        name="compress",
    )(table, *([pages] * PAGES_PER_STEP), w1bd, bias2, w2bd)


def _overlap(n_cmp_rows, n_sel_cols):
    c0 = jnp.arange(n_cmp_rows) * CMP_STRIDE
    s0 = jnp.arange(n_sel_cols) * SEL_BLOCK
    ov = jnp.minimum(c0[:, None] + CMP_BLOCK, s0[None, :] + SEL_BLOCK) - jnp.maximum(c0[:, None], s0[None, :])
    return (jnp.clip(ov, 0, None).astype(F32) / CMP_BLOCK).astype(BF16)


def _block_expand(n_blocks, n_keys):
    return (jnp.arange(n_keys)[None, :] // SEL_BLOCK == jnp.arange(n_blocks)[:, None]).astype(BF16)


def _masked_softmax(s, mask, axis=-1):
    m = jnp.max(jnp.where(mask, s, NEG), axis=axis, keepdims=True)
    e = jnp.where(mask, jnp.exp(s - m), 0.0)
    d = jnp.sum(e, axis=axis, keepdims=True)
    return e / jnp.where(d > 0, d, 1.0)


def _top_blocks(score, idx, axis):
    n = score.shape[axis]
    sel = jnp.zeros(score.shape, F32)
    work = score
    for _ in range(N_SEL):
        m = jnp.max(work, axis=axis, keepdims=True)
        first = jnp.min(jnp.where(work == m, idx, n), axis=axis, keepdims=True)
        pick = idx == first
        sel = jnp.where(pick & (m > -jnp.inf), 1.0, sel)
        work = jnp.where(pick, -jnp.inf, work)
    return sel


SEL_TK = 512
WIN_KEYS = WINDOW + Q_BLOCK
LOG2E = 1.4426950408889634
V_ROWS = HEAD_DIM_A + 16


def _col_max(s):
    n = s.shape[0]
    parts = jnp.max(s.reshape(SUBLANES, n // SUBLANES, s.shape[1]), axis=0)
    return jnp.max(parts, axis=0, keepdims=True)


def _kv_split_kernel(kv_ref, k_ref, vt_ref):
    k_ref[...] = kv_ref[:, :KV_W].astype(BF16)
    vt = kv_ref[:, KV_W:].T
    tt = vt.shape[1]
    pad_rows = lax.broadcasted_iota(jnp.int32, (V_ROWS - HEAD_DIM_A, tt), 0)
    ones_then_zeros = jnp.where(pad_rows == 0, 1.0, 0.0).astype(BF16)
    for k in range(N_KV_A):
        vt_ref[k * V_ROWS:k * V_ROWS + HEAD_DIM_A, :] = vt[k * HEAD_DIM_A:(k + 1) * HEAD_DIM_A, :].astype(BF16)
        vt_ref[k * V_ROWS + HEAD_DIM_A:(k + 1) * V_ROWS, :] = ones_then_zeros


def _kv_split(z, col, *, tt):
    B, T, _ = z.shape
    return pl.pallas_call(
        _kv_split_kernel,
        out_shape=(jax.ShapeDtypeStruct((B, T, KV_W), BF16), jax.ShapeDtypeStruct((B, N_KV_A * V_ROWS, T), BF16)),
        grid=(B, T // tt),
        in_specs=[pl.BlockSpec((None, tt, KV_BRANCH), lambda b, i: (b, i, col // KV_BRANCH))],
        out_specs=(pl.BlockSpec((None, tt, KV_W), lambda b, i: (b, i, 0)),
                   pl.BlockSpec((None, N_KV_A * V_ROWS, tt), lambda b, i: (b, 0, i))),
        compiler_params=_cparams(("parallel", "parallel")),
        name="kv_split",
    )(z)


def _values_t(v):
    B, n, _ = v.shape
    vt = jnp.swapaxes(v, 1, 2).reshape(B, N_KV_A, HEAD_DIM_A, n)
    pad = jnp.zeros((B, N_KV_A, V_ROWS - HEAD_DIM_A, n), F32).at[:, :, 0].set(1.0)
    return jnp.concatenate([vt, pad], axis=2).reshape(B, N_KV_A * V_ROWS, n).astype(BF16)


def _nsa_prompt_kernel(q_ref, ga_ref, g_ref, ck_ref, cvt_ref, sk_ref, svt_ref, wk_ref, wvt_ref, et_ref, ovt_ref,
                       o_ref, qt_scr, ot_scr, qb_scr, m_scr, acc_scr, s_scr):
    i = pl.program_id(1)
    q0 = i * Q_BLOCK
    n_cmp = ck_ref.shape[0]
    pos_q = q0 + lax.broadcasted_iota(jnp.int32, (1, Q_BLOCK), 1)
    for c in range(D_A // LANES):
        cols = slice(c * LANES, (c + 1) * LANES)
        qt_scr[cols, :] = (q_ref[:, cols].T * (HEAD_DIM_A ** -0.5 * LOG2E)).astype(BF16)
    gates_t = jax.nn.sigmoid(g_ref[...]).T
    j_t = lax.broadcasted_iota(jnp.int32, (LANES, Q_BLOCK), 0)
    cur_t = pos_q // SEL_BLOCK
    valid_t = j_t * SEL_BLOCK <= pos_q
    forced_t = (j_t == 0) | (j_t == cur_t) | (j_t == cur_t - 1)
    cmp_end = lax.broadcasted_iota(jnp.int32, (n_cmp, 1), 0) * CMP_STRIDE + (CMP_BLOCK - 1)
    cbias = jnp.where(cmp_end <= pos_q, 0.0, NEG)
    cvalid = jnp.where(pos_q >= CMP_BLOCK - 1, 1.0, 0.0)
    w0 = pl.multiple_of(jnp.maximum(q0 - WINDOW, 0), Q_BLOCK)
    pos_w = w0 + lax.broadcasted_iota(jnp.int32, (WIN_KEYS, 1), 0)
    wbias = jnp.where((pos_w <= pos_q) & (pos_q - pos_w < WINDOW), 0.0, NEG)
    n_full = q0 // SEL_TK
    zero_half = jnp.zeros((HEAD_DIM_A, Q_BLOCK), BF16)

    def weights_t(s, bias):
        out = []
        for g in range(GROUP_A):
            sg = s[:, g * Q_BLOCK:(g + 1) * Q_BLOCK] + bias
            out.append(jnp.exp2(sg - jnp.max(sg, axis=0, keepdims=True)).astype(BF16))
        return jnp.concatenate(out, axis=1)

    def normalise(o_ext, scale=None):
        d = o_ext[HEAD_DIM_A:HEAD_DIM_A + 1, :]
        inv = 1.0 / jnp.where(d > 0, d, 1.0)
        return inv if scale is None else inv * scale

    def head_scores(k):
        pr = slice((k // 2) * LANES, (k // 2 + 1) * LANES)
        blocks = []
        for g in range(GROUP_A):
            h = k * GROUP_A + g
            qh = qt_scr[h * HEAD_DIM_A:(h + 1) * HEAD_DIM_A, :]
            blocks.append(jnp.concatenate([qh, zero_half] if k % 2 == 0 else [zero_half, qh], axis=0))
        qt = jnp.concatenate(blocks, axis=1)
        return (qt, jnp.dot(ck_ref[:, pr], qt, preferred_element_type=F32),
                jnp.dot(wk_ref[pl.ds(w0, WIN_KEYS), pr], qt, preferred_element_type=F32))

    ahead = head_scores(0)
    for k in range(N_KV_A):
        vr = slice(k * V_ROWS, (k + 1) * V_ROWS)
        qt, s, s_win = ahead
        if k + 1 < N_KV_A:
            ahead = head_scores(k + 1)

        e = weights_t(s, cbias)
        o_ext = jnp.dot(cvt_ref[vr, :], e, preferred_element_type=F32)
        inv = normalise(o_ext, jnp.concatenate([cvalid] * GROUP_A, axis=1))
        o_cmp = o_ext[:HEAD_DIM_A] * inv
        pj = jnp.dot(ovt_ref[...], e, preferred_element_type=F32) * inv
        p_slc = pj[:, 0:Q_BLOCK]
        for g in range(1, GROUP_A):
            p_slc = p_slc + pj[:, g * Q_BLOCK:(g + 1) * Q_BLOCK]

        score_t = jnp.where(valid_t, jnp.where(forced_t, FORCE_SCORE, p_slc), -jnp.inf)
        sel_bias = ((_top_blocks(score_t, j_t, 0) - 1.0) * (-NEG)).astype(BF16)
        qb = jnp.concatenate([qt, jnp.concatenate([sel_bias] * GROUP_A, axis=1)], axis=0)

        qb_scr[k] = qb

        o_ext = jnp.dot(wvt_ref[vr, pl.ds(w0, WIN_KEYS)], weights_t(s_win, wbias), preferred_element_type=F32)
        o_win = o_ext[:HEAD_DIM_A] * normalise(o_ext)

        for g in range(GROUP_A):
            h = k * GROUP_A + g
            lanes = slice(g * Q_BLOCK, (g + 1) * Q_BLOCK)
            ot_scr[h * HEAD_DIM_A:(h + 1) * HEAD_DIM_A, :] = (
                gates_t[h:h + 1, :] * o_cmp[:, lanes]
                + gates_t[2 * N_HEADS_A + h:2 * N_HEADS_A + h + 1, :] * o_win[:, lanes])

    m_scr[...] = jnp.full(m_scr.shape, NEG, F32)
    acc_scr[...] = jnp.zeros(acc_scr.shape, F32)

    def scores(k, t):
        k0 = pl.multiple_of(t * SEL_TK, SEL_TK)
        pr = slice((k // 2) * LANES, (k // 2 + 1) * LANES)
        kb = jnp.concatenate([sk_ref[pl.ds(k0, SEL_TK), pr], et_ref[pl.ds(k0, SEL_TK), :]], axis=1)
        return jnp.dot(kb, qb_scr[k], preferred_element_type=F32)

    n_ahead = s_scr.shape[0]
    for k in range(n_ahead):
        s_scr[k] = scores(k, 0)

    def sel_step(t, causal):
        k0 = pl.multiple_of(t * SEL_TK, SEL_TK)
        if causal:
            pos_k = k0 + lax.broadcasted_iota(jnp.int32, (SEL_TK, 1), 0)
            future = jnp.where(pos_k <= pos_q, 0.0, NEG)
        s_own = {}
        for k in range(N_KV_A):
            ahead = k + n_ahead
            if ahead < N_KV_A:
                s_own[ahead] = scores(ahead, t)
            elif not causal:
                s_scr[ahead - N_KV_A] = scores(ahead - N_KV_A, t + 1)
            s = s_scr[k] if k < n_ahead else s_own.pop(k)
            es, alphas = [], []
            for g in range(GROUP_A):
                lanes = slice(g * Q_BLOCK, (g + 1) * Q_BLOCK)
                sg = s[:, lanes] + future if causal else s[:, lanes]
                m_old = m_scr[k, :, lanes]
                m_new = jnp.maximum(m_old, _col_max(sg))
                alphas.append(jnp.exp2(m_old - m_new))
                es.append(jnp.exp2(sg - m_new).astype(BF16))
                m_scr[k, :, lanes] = m_new
            pv = jnp.dot(svt_ref[k * V_ROWS:(k + 1) * V_ROWS, pl.ds(k0, SEL_TK)], jnp.concatenate(es, axis=1),
                         preferred_element_type=F32)
            acc_scr[k] = jnp.concatenate(alphas, axis=1) * acc_scr[k] + pv

    def full_step(t, carry):
        sel_step(t, False)
        return carry

    lax.fori_loop(0, n_full, full_step, 0)
    sel_step(n_full, True)
    for k in range(N_KV_A):
        acc = acc_scr[k]
        o_sel = acc[:HEAD_DIM_A] * normalise(acc)
        for g in range(GROUP_A):
            h = k * GROUP_A + g
            lanes = slice(g * Q_BLOCK, (g + 1) * Q_BLOCK)
            rows = slice(h * HEAD_DIM_A, (h + 1) * HEAD_DIM_A)
            ot_scr[rows, :] = ot_scr[rows, :] + gates_t[N_HEADS_A + h:N_HEADS_A + h + 1, :] * o_sel[:, lanes]

    for c in range(D_A // LANES):
        cols = slice(c * LANES, (c + 1) * LANES)
        o_ref[:, cols] = (ot_scr[cols, :].T * _silu(ga_ref[:, cols])).astype(o_ref.dtype)


def _nsa_prompt(z, ck, cvt, sk, svt, wk, wvt):
    B, T, _ = z.shape
    n_cmp = ck.shape[1]
    et = _block_expand(LANES, T).T
    ovt = _overlap(n_cmp, LANES).T
    once = pl.Buffered(1)
    gq = GROUP_A * Q_BLOCK
    vt_w = N_KV_A * V_ROWS

    def whole(shape):
        return pl.BlockSpec((None,) + shape, lambda b, i: (b, 0, 0), pipeline_mode=once)

    return pl.pallas_call(
        _nsa_prompt_kernel,
        out_shape=jax.ShapeDtypeStruct((B, T, D_A), BF16),
        grid=(B, T // Q_BLOCK),
        in_specs=[pl.BlockSpec((None, Q_BLOCK, D_A), lambda b, i: (b, i, COL_Q // D_A)),
                  pl.BlockSpec((None, Q_BLOCK, D_A), lambda b, i: (b, i, COL_GA // D_A)),
                  pl.BlockSpec((None, Q_BLOCK, LANES), lambda b, i: (b, i, COL_G // LANES)),
                  whole((n_cmp, KV_W)), whole((vt_w, n_cmp)),
                  whole((T, KV_W)), whole((vt_w, T)), whole((T, KV_W)), whole((vt_w, T)),
                  pl.BlockSpec((T, LANES), lambda b, i: (0, 0), pipeline_mode=once),
                  pl.BlockSpec((LANES, n_cmp), lambda b, i: (0, 0), pipeline_mode=once)],
        out_specs=pl.BlockSpec((None, Q_BLOCK, D_A), lambda b, i: (b, i, 0)),
        scratch_shapes=[pltpu.VMEM((D_A, Q_BLOCK), BF16), pltpu.VMEM((D_A, Q_BLOCK), F32),
                        pltpu.VMEM((N_KV_A, 2 * LANES, gq), BF16),
                        pltpu.VMEM((N_KV_A, 1, gq), F32), pltpu.VMEM((N_KV_A, V_ROWS, gq), F32),
                        pltpu.VMEM((2, SEL_TK, gq), F32)],
        compiler_params=_cparams(("parallel", "arbitrary")),
        name="nsa_prompt",
    )(z, z, z, ck, cvt, sk, svt, wk, wvt, et, ovt)


def _nsa_sample_kernel(tab_ref, *refs, past, n_steps, t_new):
    pages = refs[:PAGES_PER_STEP]
    (q_ref, ckt_ref, cv_ref, new_ref, wkv_ref, g_ref, ga_ref, e_ref, ov_ref, o_ref, hist) = refs[PAGES_PER_STEP:]
    s = pl.program_id(1)
    for r in range(PAGES_PER_STEP):
        col = pl.multiple_of((s * PAGES_PER_STEP + r) * PAGE_SIZE, PAGE_SIZE)
        hist[:, pl.ds(col, PAGE_SIZE)] = pages[r][...]

    @pl.when(s == 0)
    def _():
        hist[:, past:] = new_ref[...]

    @pl.when(s == n_steps - 1)
    def _():
        n_rows = N_HEADS_A * t_new
        n_keys = hist.shape[1]
        n_cmp = ckt_ref.shape[1]
        n_blk = ov_ref.shape[1]
        n_win = wkv_ref.shape[0]
        q = q_ref[...]
        row = lax.broadcasted_iota(jnp.int32, (n_rows, 1), 0)
        pos_q = past + row % t_new
        row_k = (row // t_new) % N_KV_A

        def diag(o_all):
            out = jnp.zeros((n_rows, HEAD_DIM_A), F32)
            for k in range(N_KV_A):
                out = out + jnp.where(row_k == k, o_all[:, k * HEAD_DIM_A:(k + 1) * HEAD_DIM_A], 0.0)
            return out

        cmp_end = lax.broadcasted_iota(jnp.int32, (1, n_cmp), 1) * CMP_STRIDE + (CMP_BLOCK - 1)
        s_c = jnp.dot(q, ckt_ref[...], preferred_element_type=F32)
        p = _masked_softmax(s_c, cmp_end <= pos_q).astype(BF16)
        o_cmp = diag(jnp.dot(p, cv_ref[...], preferred_element_type=F32))
        pj = jnp.dot(p, ov_ref[...], preferred_element_type=F32)
        rows_kq = N_KV_A * t_new
        p_slc = pj[0:rows_kq]
        for g in range(1, GROUP_A):
            p_slc = p_slc + pj[g * rows_kq:(g + 1) * rows_kq]

        j = lax.broadcasted_iota(jnp.int32, (rows_kq, n_blk), 1)
        pos_r = past + lax.broadcasted_iota(jnp.int32, (rows_kq, n_blk), 0) % t_new
        cur = pos_r // SEL_BLOCK
        valid = j * SEL_BLOCK <= pos_r
        forced = (j == 0) | (j == cur) | (j == cur - 1)
        score = jnp.where(valid, jnp.where(forced, FORCE_SCORE, p_slc), -jnp.inf)
        sel = _top_blocks(score, j, 1)
        sel = jnp.concatenate([sel] * GROUP_A, axis=0).astype(BF16)

        picked = jnp.dot(sel, e_ref[...], preferred_element_type=F32)
        s_s = jnp.dot(q, hist[0:KV_W, :].astype(BF16), preferred_element_type=F32)
        pos_k = lax.broadcasted_iota(jnp.int32, (1, n_keys), 1)
        p = _masked_softmax(s_s, (picked > 0.5) & (pos_k <= pos_q)).astype(BF16)
        o_sel = diag(lax.dot_general(p, hist[KV_W:, :].astype(BF16), (((1,), (1,)), ((), ())),
                                     preferred_element_type=F32))

        wk = wkv_ref[:, 0:KV_W].astype(BF16)
        s_w = lax.dot_general(q, wk, (((1,), (1,)), ((), ())), preferred_element_type=F32)
        pos_w = past - WINDOW + lax.broadcasted_iota(jnp.int32, (1, n_win), 1)
        p = _masked_softmax(s_w, (pos_w <= pos_q) & (pos_q - pos_w < WINDOW)).astype(BF16)
        o_win = diag(jnp.dot(p, wkv_ref[:, KV_W:].astype(BF16), preferred_element_type=F32))

        gates = jax.nn.sigmoid(g_ref[...])
        o = gates[:, 0:1] * o_cmp + gates[:, 1:2] * o_sel + gates[:, 2:3] * o_win
        o_ref[...] = o * _silu(ga_ref[...])


def _nsa_sample(sel_view, table, qbs, ckt, cv, new_sel, wkv, g_rows, ga_rows, *, past, t_new):
    NB, n_pages = table.shape
    n_steps = n_pages // PAGES_PER_STEP
    n_keys = past + PAGE_SIZE
    n_cmp = ckt.shape[2]
    n_blk = 2 * LANES
    n_rows = qbs.shape[1]
    e = _block_expand(n_blk, n_keys)
    ov = _overlap(n_cmp, n_blk)

    def page_spec(r):
        return pl.BlockSpec((None, KV_BRANCH, PAGE_SIZE), lambda b, s, tab: (tab[b, s * PAGES_PER_STEP + r], 0, 0))

    def per_b(arr):
        return pl.BlockSpec((None,) + arr.shape[1:], lambda b, s, tab: (b, 0, 0))

    const2 = lambda b, s, tab: (0, 0)
    return pl.pallas_call(
        functools.partial(_nsa_sample_kernel, past=past, n_steps=n_steps, t_new=t_new),
        out_shape=jax.ShapeDtypeStruct((NB, n_rows, HEAD_DIM_A), F32),
        grid_spec=pltpu.PrefetchScalarGridSpec(
            num_scalar_prefetch=1,
            grid=(NB, n_steps),
            in_specs=[page_spec(r) for r in range(PAGES_PER_STEP)] + [
                per_b(qbs), per_b(ckt), per_b(cv), per_b(new_sel), per_b(wkv), per_b(g_rows), per_b(ga_rows),
                pl.BlockSpec(e.shape, const2, pipeline_mode=pl.Buffered(1)),
                pl.BlockSpec(ov.shape, const2, pipeline_mode=pl.Buffered(1))],
            out_specs=pl.BlockSpec((None, n_rows, HEAD_DIM_A), lambda b, s, tab: (b, 0, 0)),
            scratch_shapes=[pltpu.VMEM((KV_BRANCH, n_keys), F32)]),
        compiler_params=_cparams(("parallel", "arbitrary")),
        name="nsa_sample",
    )(table, *([sel_view] * PAGES_PER_STEP), qbs, ckt, cv, new_sel, wkv, g_rows, ga_rows, e, ov)


def _permute_w_in_a(w):
    c_kv, c_g, c_ga, c_glu, c_gb = 1024, 2560, 2608, 3632, 5680
    cols = [w[:, :c_kv], w[:, c_ga:c_glu], w[:, c_glu:c_gb], w[:, c_gb:], w[:, c_kv:c_g], w[:, c_g:c_ga]]
    n = sum(c.shape[1] for c in cols)
    return jnp.concatenate(cols + [jnp.zeros((w.shape[0], EVEN_N - n), w.dtype)], axis=1).astype(BF16)


def _pad_rows(x, front, total):
    return jnp.pad(x, ((0, 0), (front, total - front - x.shape[1]), (0, 0)))


def _even_prompt(z, cmpw, conv_args, *, tq):
    B, T, _ = z.shape
    kv = z[:, :, COL_KV:COL_G]
    kv_cmp, kv_sel, kv_win = (kv[:, :, br * KV_BRANCH:(br + 1) * KV_BRANCH] for br in range(3))
    n_pages = T // PAGE_SIZE
    table = jnp.arange(B * n_pages, dtype=jnp.int32).reshape(B, n_pages)
    ckv = _compress(kv_cmp.reshape(B * n_pages, PAGE_SIZE, KV_BRANCH), table, *cmpw, transposed=False)

    def split_kv(x):
        return x[:, :, :KV_W].astype(BF16), _values_t(x[:, :, KV_W:])

    o_a = _nsa_prompt(z, *split_kv(ckv), *_kv_split(z, COL_KV + KV_BRANCH, tt=tq),
                      *_kv_split(z, COL_KV + 2 * KV_BRANCH, tt=tq))
    u = _glu(z, tt=tq)
    halo_map = lambda bb, i: (bb, jnp.maximum(i * (tq // CONV_HALO) - 1, 0), 0)
    o_b = _conv(u, halo_map, True, u, z, COL_GB // D_MODEL, *conv_args, tt=tq)
    kv_shape = (B, T, 2, N_KV_A, HEAD_DIM_A)
    w_keep = min(WINDOW, T)
    outs = (kv_cmp.reshape(kv_shape), kv_sel.reshape(kv_shape),
            kv_win[:, T - w_keep:].reshape(B, w_keep, 2, N_KV_A, HEAD_DIM_A), u[:, T - (CONV_WIDTH - 1):])
    return o_a, o_b, outs


def _even_sample(z, cmpw, conv_args, cmp_view, sel_view, table, win_buf, conv_hist, *, past):
    NB, t_new, _ = z.shape
    kv = z[:, :, COL_KV:COL_G]
    kv_cmp, kv_sel, kv_win = (kv[:, :, br * KV_BRANCH:(br + 1) * KV_BRANCH] for br in range(3))
    ckv = _compress(cmp_view, table, *cmpw, transposed=True)
    q = z[:, :, COL_Q:COL_Q + D_A].reshape(NB, t_new, N_KV_A, GROUP_A, HEAD_DIM_A) * (HEAD_DIM_A ** -0.5)
    qbs = jnp.einsum('bqkgd,kK->bgkqKd', q, jnp.eye(N_KV_A, dtype=F32)).reshape(NB, N_HEADS_A * t_new, KV_W)
    g_rows = z[:, :, COL_G:COL_G + 3 * N_HEADS_A].reshape(NB, t_new, 3, N_KV_A, GROUP_A)
    g_rows = jnp.transpose(g_rows, (0, 4, 3, 1, 2)).reshape(NB, N_HEADS_A * t_new, 3)
    g_rows = jnp.pad(g_rows, ((0, 0), (0, 0), (0, LANES - 3)))
    ga_rows = z[:, :, COL_GA:COL_GA + D_A].reshape(NB, t_new, N_KV_A, GROUP_A, HEAD_DIM_A)
    ga_rows = jnp.transpose(ga_rows, (0, 3, 2, 1, 4)).reshape(NB, N_HEADS_A * t_new, HEAD_DIM_A)
    wkv = jnp.concatenate([win_buf.reshape(NB, -1, KV_BRANCH), kv_win], axis=1)
    wb = win_buf.shape[1]
    o = _nsa_sample(sel_view, table, qbs.astype(BF16),
                    jnp.swapaxes(ckv[:, :, :KV_W], 1, 2).astype(BF16), ckv[:, :, KV_W:].astype(BF16),
                    jnp.swapaxes(_pad_rows(kv_sel, 0, PAGE_SIZE), 1, 2), _pad_rows(wkv, 0, WIN_KEYS), g_rows, ga_rows,
                    past=past, t_new=t_new)
    o_a = jnp.transpose(o.reshape(NB, GROUP_A, N_KV_A, t_new, HEAD_DIM_A), (0, 3, 2, 1, 4))
    o_a = o_a.reshape(NB * t_new, D_A).astype(BF16)
    u = _glu(z.reshape(1, NB * t_new, EVEN_N), tt=NB * t_new).reshape(NB, t_new, D_MODEL)
    halo = _pad_rows(conv_hist, CONV_HALO - (CONV_WIDTH - 1), CONV_HALO)
    gate_b = _pad_rows(z[:, :, COL_GB:COL_GB + D_MODEL], 0, SUBLANES)
    o_b = _conv(halo, lambda bb, i: (bb, 0, 0), False, _pad_rows(u, 0, SUBLANES), gate_b, 0, *conv_args, tt=SUBLANES)
    o_b = o_b[:, :t_new].reshape(NB * t_new, D_MODEL)
    kv_shape = (NB, t_new, 2, N_KV_A, HEAD_DIM_A)
    outs = (kv_cmp.reshape(kv_shape), kv_sel.reshape(kv_shape),
            wkv[:, t_new:].reshape(NB, wb, 2, N_KV_A, HEAD_DIM_A),
            jnp.concatenate([conv_hist, u], axis=1)[:, -(CONV_WIDTH - 1):])
    return o_a, o_b, outs


def kernel(x_prompt, x_sample, mem_prompt, cache_cmp_kv, cache_sel_kv, cache_win_kv, state_conv, state_pool,
           cache_mem_kv, page_table, norm_mix, norm_mem, norm_out, w_in_a, w_out_a, cmp_pe, cmp_w1, cmp_w2,
           conv_w, conv_b, conv_ln_g, conv_ln_b, w_in_c, pool_w, pool_scale, w_out_c, w_mem_q, w_mem_kv, w_mem_o):
    B, T, D = x_prompt.shape
    NB, t_new, _ = x_sample.shape
    depth = norm_mix.shape[0]
    n_pool = cache_cmp_kv.shape[1]
    past = page_table.shape[1] * PAGE_SIZE
    mem_len = mem_prompt.shape[1]
    tm_p = 1024
    tq = 256
    ms = NB * t_new
    xp = x_prompt.reshape(B * T, D)
    xs = x_sample.reshape(ms, D)
    p_cmp, p_sel, p_win, p_conv, p_pool, p_mem = [], [], [], [], [], []
    s_cmp, s_sel, s_win, s_conv, s_pool = [], [], [], [], []

    def pages_t(cache):
        return jnp.transpose(cache, (0, 1, 3, 4, 5, 2)).reshape(-1, KV_BRANCH, PAGE_SIZE)

    cmp_view = pages_t(cache_cmp_kv)
    sel_view = pages_t(cache_sel_kv)

    for layer in range(depth):
        if layer % 2 == 0:
            a = layer // 2
            w_in = _permute_w_in_a(w_in_a[a])
            w_out = w_out_a[a].astype(BF16)
            cmpw = _compress_weights(cmp_pe[a], cmp_w1[a], cmp_w2[a])
            conv_args = (conv_w[a], conv_b[a], conv_ln_g[a], conv_ln_b[a])
            zp = _mm([xp], w_in, gain=norm_mix[layer], tm=tm_p, tn=EVEN_TN).reshape(B, T, EVEN_N)
            o_a, o_b, outs = _even_prompt(zp, cmpw, conv_args, tq=tq)
            xp = _mm([o_a.reshape(B * T, D_A), o_b.reshape(B * T, D)], w_out, res=xp, tm=tm_p, tn=D)
            for lst, val in zip((p_cmp, p_sel, p_win, p_conv), outs):
                lst.append(val)
            zs = _mm([xs], w_in, gain=norm_mix[layer], tm=ms, tn=EVEN_TN).reshape(NB, t_new, EVEN_N)
            table = page_table + a * n_pool
            o_a, o_b, outs = _even_sample(zs, cmpw, conv_args, cmp_view, sel_view, table, cache_win_kv[a],
                                          state_conv[a], past=past)
            xs = _mm([o_a, o_b], w_out, res=xs, tm=ms, tn=D)
            for lst, val in zip((s_cmp, s_sel, s_win, s_conv), outs):
                lst.append(val)
        else:
            c = layer // 2
            w_in = w_in_c[c].astype(BF16)
            w_out = w_out_c[c].astype(BF16)
            zp = _mm([xp], w_in, gain=norm_mix[layer], tm=tm_p, tn=D).reshape(B, T, 2 * D)
            halo_map = lambda bb, i: (bb, jnp.maximum(i * (tq // POOL_HALO) - 1, 0), 0)
            m = _pool(zp, halo_map, True, zp, pool_w[c], pool_scale[c], tt=tq, pos0=0)
            xp = _mm([m.reshape(B * T, D)], w_out, res=xp, tm=tm_p, tn=D)
            p_pool.append(zp[:, T - POOL_HIST:, :D])
            zs = _mm([xs], w_in, gain=norm_mix[layer], tm=ms, tn=D).reshape(NB, t_new, 2 * D)
            hist = state_pool[c]
            m = _pool(_pad_rows(hist, POOL_HALO - POOL_HIST, POOL_HALO), lambda bb, i: (bb, 0, 0), False,
                      _pad_rows(zs, 0, SUBLANES), pool_w[c], pool_scale[c], tt=SUBLANES, pos0=past)
            xs = _mm([m[:, :t_new].reshape(ms, D)], w_out, res=xs, tm=ms, tn=D)
            s_pool.append(jnp.concatenate([hist, zs[:, :, :D]], axis=1)[:, -POOL_HIST:])

        wq = w_mem_q[layer].astype(BF16)
        wo = w_mem_o[layer].astype(BF16)
        mkv = _mm([mem_prompt.reshape(B * mem_len, D)], w_mem_kv[layer].astype(BF16), tm=B * mem_len, tn=D)
        mkv = mkv.reshape(B, mem_len, 2 * D)
        p_mem.append(mkv.reshape(B, mem_len, 2, N_HEADS_M, HEAD_DIM_M))
        q = _mm([xp], wq, gain=norm_mem[layer], tm=tm_p, tn=D, out_dtype=BF16).reshape(B, T, D)
        o = _mem_attn(q, jnp.swapaxes(mkv[:, :, :D], 1, 2).astype(BF16), mkv[:, :, D:].astype(BF16), tm=tq)
        xp = _mm([o.reshape(B * T, D)], wo, res=xp, tm=tm_p, tn=D)
        ckv = cache_mem_kv[layer].reshape(NB, mem_len, 2 * D)
        q = _mm([xs], wq, gain=norm_mem[layer], tm=ms, tn=D, out_dtype=BF16).reshape(NB, t_new, D)
        o = _mem_attn(_pad_rows(q, 0, SUBLANES), jnp.swapaxes(ckv[:, :, :D], 1, 2).astype(BF16),
                      ckv[:, :, D:].astype(BF16), tm=SUBLANES)
        xs = _mm([o[:, :t_new].reshape(ms, D)], wo, res=xs, tm=ms, tn=D)

    y_prompt = _rmsnorm(xp, norm_out, tm=tm_p).reshape(B, T, D)
    y_sample = _rmsnorm(xs, norm_out, tm=ms).reshape(NB, t_new, D)
    return (y_prompt, y_sample,
            jnp.stack(p_cmp), jnp.stack(p_sel), jnp.stack(p_win), jnp.stack(p_conv), jnp.stack(p_pool),
            jnp.stack(p_mem),
            jnp.stack(s_cmp), jnp.stack(s_sel), jnp.stack(s_win), jnp.stack(s_conv), jnp.stack(s_pool))
```

```python
import functools

import jax
import jax.numpy as jnp
from jax import lax
from jax.experimental import pallas as pl
from jax.experimental.pallas import tpu as pltpu

F32 = jnp.float32
BF16 = jnp.bfloat16

LANES = 128
SUBLANES = 8
VMEM_LIMIT_BYTES = 56 * 1024 * 1024

D_MODEL = 1024
N_HEADS_A = 16
HEAD_DIM_A = 64
N_KV_A = 4
GROUP_A = N_HEADS_A // N_KV_A
D_A = N_HEADS_A * HEAD_DIM_A
KV_W = N_KV_A * HEAD_DIM_A
KV_BRANCH = 2 * KV_W
CMP_STRIDE = 16
CMP_BLOCK = 32
CMP_HIDDEN = 128
SEL_BLOCK = 64
N_SEL = 16
WINDOW = 512
Q_BLOCK = 128
FORCE_SCORE = 1000.0
CONV_WIDTH = 31
CONV_HALO = 32
POOL_WINDOWS = (2, 4, 8, 16)
POOL_GROUP = D_MODEL // len(POOL_WINDOWS)
POOL_HIST = 15
POOL_HALO = 16
PAGE_SIZE = 128
N_HEADS_M = 4
HEAD_DIM_M = 256
NORM_EPS = 1e-6
NEG = -1e30

COL_Q, COL_GA, COL_VAL, COL_GL, COL_GB = 0, 1024, 2048, 3072, 4096
COL_KV = 5120
COL_G = COL_KV + 3 * KV_BRANCH
EVEN_N = 6912
EVEN_TN = 768
PAGES_PER_STEP = 32


def _cparams(sem):
    return pltpu.CompilerParams(dimension_semantics=sem, vmem_limit_bytes=VMEM_LIMIT_BYTES)


def _silu(x):
    return x * jax.nn.sigmoid(x)


def _mm_kernel(*refs, n_parts, has_gain, has_res):
    parts = refs[:n_parts]
    pos = n_parts
    gain_ref = refs[pos] if has_gain else None
    pos += int(has_gain)
    w_ref = refs[pos]
    pos += 1
    res_ref = refs[pos] if has_res else None
    pos += int(has_res)
    o_ref, lhs = refs[pos], refs[pos + 1]

    @pl.when(pl.program_id(1) == 0)
    def _():
        off = 0
        for p in parts:
            x = p[...].astype(F32)
            if has_gain:
                x = x * lax.rsqrt(jnp.mean(x * x, axis=-1, keepdims=True) + NORM_EPS) * gain_ref[...]
            lhs[:, off:off + p.shape[1]] = x.astype(BF16)
            off += p.shape[1]

    acc = jnp.dot(lhs[...], w_ref[...], preferred_element_type=F32)
    if has_res:
        acc = acc + res_ref[...]
    o_ref[...] = acc.astype(o_ref.dtype)


def _mm(parts, w, *, gain=None, res=None, tm, tn, out_dtype=F32):
    M = parts[0].shape[0]
    K, N = w.shape
    assert sum(p.shape[1] for p in parts) == K and M % tm == 0 and N % tn == 0
    in_specs = [pl.BlockSpec((tm, p.shape[1]), lambda i, j: (i, 0)) for p in parts]
    args = list(parts)
    if gain is not None:
        in_specs.append(pl.BlockSpec((1, K), lambda i, j: (0, 0)))
        args.append(gain.reshape(1, K).astype(F32))
    in_specs.append(pl.BlockSpec((K, tn), lambda i, j: (0, j)))
    args.append(w)
    if res is not None:
        in_specs.append(pl.BlockSpec((tm, tn), lambda i, j: (i, j)))
        args.append(res)
    return pl.pallas_call(
        functools.partial(_mm_kernel, n_parts=len(parts), has_gain=gain is not None, has_res=res is not None),
        out_shape=jax.ShapeDtypeStruct((M, N), out_dtype),
        grid=(M // tm, N // tn),
        in_specs=in_specs,
        out_specs=pl.BlockSpec((tm, tn), lambda i, j: (i, j)),
        scratch_shapes=[pltpu.VMEM((tm, K), BF16)],
        compiler_params=_cparams(("parallel", "arbitrary")),
        name="mm",
    )(*args)


def _rmsnorm_kernel(x_ref, g_ref, o_ref):
    x = x_ref[...]
    o_ref[...] = x * lax.rsqrt(jnp.mean(x * x, axis=-1, keepdims=True) + NORM_EPS) * g_ref[...]


def _rmsnorm(x, g, *, tm):
    M, D = x.shape
    return pl.pallas_call(
        _rmsnorm_kernel,
        out_shape=jax.ShapeDtypeStruct((M, D), F32),
        grid=(M // tm,),
        in_specs=[pl.BlockSpec((tm, D), lambda i: (i, 0)), pl.BlockSpec((1, D), lambda i: (0, 0))],
        out_specs=pl.BlockSpec((tm, D), lambda i: (i, 0)),
        compiler_params=_cparams(("parallel",)),
        name="rmsnorm",
    )(x, g.reshape(1, D))


def _glu_kernel(val_ref, gl_ref, o_ref):
    o_ref[...] = val_ref[...] * jax.nn.sigmoid(gl_ref[...])


def _glu(z, *, tt):
    B, T, _ = z.shape
    blk = (None, tt, D_MODEL)
    return pl.pallas_call(
        _glu_kernel,
        out_shape=jax.ShapeDtypeStruct((B, T, D_MODEL), F32),
        grid=(B, T // tt),
        in_specs=[pl.BlockSpec(blk, lambda b, i: (b, i, COL_VAL // D_MODEL)),
                  pl.BlockSpec(blk, lambda b, i: (b, i, COL_GL // D_MODEL))],
        out_specs=pl.BlockSpec(blk, lambda b, i: (b, i, 0)),
        compiler_params=_cparams(("parallel", "parallel")),
        name="glu",
    )(z, z)


CONV_ROWS = 32


def _conv_kernel(halo_ref, cur_ref, gate_ref, w_ref, b_ref, lg_ref, lb_ref, o_ref, buf, sh, y_scr, *, tt, zero_first_halo):
    halo = halo_ref[...]
    if zero_first_halo:
        halo = jnp.where(pl.program_id(1) == 0, 0.0, halo)
    buf[0:CONV_HALO, :] = halo
    buf[CONV_HALO:CONV_HALO + tt, :] = cur_ref[...]
    buf[CONV_HALO + tt:, :] = jnp.zeros((SUBLANES, D_MODEL), F32)
    first = CONV_HALO - (CONV_WIDTH - 1)
    n_sh = sh.shape[1]
    for r in range(SUBLANES):
        sh[r] = buf[first + r:first + r + n_sh, :]
    rc = min(CONV_ROWS, tt)
    for r0 in range(0, tt, rc):
        acc = jnp.broadcast_to(b_ref[...], (rc, D_MODEL))
        for j in range(CONV_WIDTH):
            a0 = r0 + (j // SUBLANES) * SUBLANES
            acc = acc + w_ref[j:j + 1, :] * sh[j % SUBLANES, a0:a0 + rc, :]
        y_scr[r0:r0 + rc, :] = acc
    y = y_scr[...]
    xc = y - jnp.mean(y, axis=-1, keepdims=True)
    yn = xc * lax.rsqrt(jnp.mean(xc * xc, axis=-1, keepdims=True) + NORM_EPS) * lg_ref[...] + lb_ref[...]
    o_ref[...] = (_silu(yn) * _silu(gate_ref[...])).astype(o_ref.dtype)


def _conv(halo_arr, halo_map, zero_first_halo, u, gate_arr, gate_col, w, b, lg, lb, *, tt):
    B, T, _ = u.shape
    w32 = jnp.concatenate([w, jnp.zeros((CONV_HALO - CONV_WIDTH + 1, D_MODEL), F32)], axis=0)[:CONV_HALO]
    vec = pl.BlockSpec((1, D_MODEL), lambda bb, i: (0, 0))
    return pl.pallas_call(
        functools.partial(_conv_kernel, tt=tt, zero_first_halo=zero_first_halo),
        out_shape=jax.ShapeDtypeStruct((B, T, D_MODEL), BF16),
        grid=(B, T // tt),
        in_specs=[pl.BlockSpec((None, CONV_HALO, D_MODEL), halo_map),
                  pl.BlockSpec((None, tt, D_MODEL), lambda bb, i: (bb, i, 0)),
                  pl.BlockSpec((None, tt, D_MODEL), lambda bb, i: (bb, i, gate_col)),
                  pl.BlockSpec((CONV_HALO, D_MODEL), lambda bb, i: (0, 0)),
                  vec, vec, vec],
        out_specs=pl.BlockSpec((None, tt, D_MODEL), lambda bb, i: (bb, i, 0)),
        scratch_shapes=[pltpu.VMEM((CONV_HALO + tt + SUBLANES, D_MODEL), F32),
                        pltpu.VMEM((SUBLANES, tt + CONV_HALO - SUBLANES, D_MODEL), F32),
                        pltpu.VMEM((tt, D_MODEL), F32)],
        compiler_params=_cparams(("parallel", "parallel")),
        name="conv",
    )(halo_arr, u, gate_arr, w32, b.reshape(1, -1), lg.reshape(1, -1), lb.reshape(1, -1))


def _pool_kernel(halo_ref, cur_ref, gate_ref, pw_ref, ps_ref, o_ref, buf, *, tt, zero_first_halo, pos0):
    i = pl.program_id(1)
    halo = halo_ref[...]
    if zero_first_halo:
        halo = jnp.where(i == 0, 0.0, halo)
    buf[0:POOL_HALO, :] = halo
    buf[POOL_HALO:POOL_HALO + tt, :] = cur_ref[...]
    pos = pos0 + i * tt + lax.broadcasted_iota(jnp.int32, (tt, 1), 0)
    for g, w in enumerate(POOL_WINDOWS):
        c0, c1 = g * POOL_GROUP, (g + 1) * POOL_GROUP
        s = buf[POOL_HALO:POOL_HALO + tt, c0:c1]
        for j in range(1, w):
            s = s + buf[POOL_HALO - j:POOL_HALO - j + tt, c0:c1]
        cnt = jnp.minimum(w, pos + 1).astype(F32)
        d = s / cnt - buf[POOL_HALO:POOL_HALO + tt, c0:c1]
        m = jnp.dot(d.astype(BF16), pw_ref[g], preferred_element_type=F32)
        o_ref[:, c0:c1] = (m * ps_ref[:, c0:c1] * _silu(gate_ref[:, c0:c1])).astype(o_ref.dtype)


def _pool(halo_arr, halo_map, zero_first_halo, z, pool_w, pool_scale, *, tt, pos0):
    B, T, _ = z.shape
    return pl.pallas_call(
        functools.partial(_pool_kernel, tt=tt, zero_first_halo=zero_first_halo, pos0=pos0),
        out_shape=jax.ShapeDtypeStruct((B, T, D_MODEL), BF16),
        grid=(B, T // tt),
        in_specs=[pl.BlockSpec((None, POOL_HALO, D_MODEL), halo_map),
                  pl.BlockSpec((None, tt, D_MODEL), lambda bb, i: (bb, i, 0)),
                  pl.BlockSpec((None, tt, D_MODEL), lambda bb, i: (bb, i, 1)),
                  pl.BlockSpec((len(POOL_WINDOWS), POOL_GROUP, POOL_GROUP), lambda bb, i: (0, 0, 0)),
                  pl.BlockSpec((1, D_MODEL), lambda bb, i: (0, 0))],
        out_specs=pl.BlockSpec((None, tt, D_MODEL), lambda bb, i: (bb, i, 0)),
        scratch_shapes=[pltpu.VMEM((POOL_HALO + tt, D_MODEL), F32)],
        compiler_params=_cparams(("parallel", "parallel")),
        name="pool",
    )(halo_arr, z, z, pool_w.astype(BF16), pool_scale.reshape(1, -1))


def _mem_attn_kernel(q_ref, kt_ref, v_ref, o_ref):
    for h in range(N_HEADS_M):
        c0, c1 = h * HEAD_DIM_M, (h + 1) * HEAD_DIM_M
        q = q_ref[:, c0:c1] * (HEAD_DIM_M ** -0.5)
        s = jnp.dot(q, kt_ref[c0:c1, :], preferred_element_type=F32)
        e = jnp.exp(s - jnp.max(s, axis=-1, keepdims=True))
        p = e / jnp.sum(e, axis=-1, keepdims=True)
        o = jnp.dot(p.astype(BF16), v_ref[:, c0:c1], preferred_element_type=F32)
        o_ref[:, c0:c1] = o.astype(o_ref.dtype)


def _mem_attn(q, kt, v, *, tm):
    NB, Tq, D = q.shape
    M = v.shape[1]
    return pl.pallas_call(
        _mem_attn_kernel,
        out_shape=jax.ShapeDtypeStruct((NB, Tq, D), BF16),
        grid=(NB, Tq // tm),
        in_specs=[pl.BlockSpec((None, tm, D), lambda b, i: (b, i, 0)),
                  pl.BlockSpec((None, D, M), lambda b, i: (b, 0, 0)),
                  pl.BlockSpec((None, M, D), lambda b, i: (b, 0, 0))],
        out_specs=pl.BlockSpec((None, tm, D), lambda b, i: (b, i, 0)),
        compiler_params=_cparams(("parallel", "parallel")),
        name="mem_attn",
    )(q, kt, v)


def _compress_weights(pe, w1, w2):
    eye2 = jnp.eye(2, dtype=F32)

    def bd(wc):
        return jnp.einsum('ldh,kK->lkdKh', wc, eye2).reshape(CMP_STRIDE * LANES, 2 * CMP_HIDDEN)

    w1bd = jnp.stack([jnp.concatenate([bd(w1[c, :CMP_STRIDE]), bd(w1[c, CMP_STRIDE:])], axis=1)
                      for c in range(2)]).astype(BF16)
    w2bd = jnp.stack([jnp.einsum('hd,kK->khKd', w2[c], eye2).reshape(2 * CMP_HIDDEN, LANES)
                      for c in range(2)]).astype(BF16)
    pe_rows = jnp.zeros((2, 2 * SUBLANES, CMP_BLOCK * HEAD_DIM_A), F32).at[:, 0].set(pe.reshape(2, -1))
    bias = jnp.stack([_mm([pe_rows[c]], w1[c].reshape(CMP_BLOCK * HEAD_DIM_A, CMP_HIDDEN).astype(BF16),
                          tm=2 * SUBLANES, tn=CMP_HIDDEN)[0] for c in range(2)])
    bias2 = jnp.concatenate([bias, bias], axis=1).reshape(2, 1, 2 * CMP_HIDDEN)
    return w1bd, w2bd, bias2


def _compress_kernel(tab_ref, *refs, n_pos, n_steps, transposed):
    pages = refs[:PAGES_PER_STEP]
    w1_ref, bias_ref, w2_ref, o_ref, hist, sec = refs[PAGES_PER_STEP:]
    s = pl.program_id(1)
    n_grp = KV_BRANCH // LANES
    per_page = PAGE_SIZE // CMP_STRIDE
    n_rows = n_pos // CMP_STRIDE
    out_row = lax.broadcasted_iota(jnp.int32, (PAGE_SIZE, PAGE_SIZE), 0)
    in_pos = lax.broadcasted_iota(jnp.int32, (PAGE_SIZE, PAGE_SIZE), 1)
    perm = (in_pos == (out_row % per_page) * CMP_STRIDE + out_row // per_page).astype(BF16)
    for r in range(PAGES_PER_STEP):
        row = pl.multiple_of((s * PAGES_PER_STEP + r) * per_page, per_page)
        page = pages[r][...].astype(BF16)
        if transposed:
            x = lax.dot_general(perm, page, (((1,), (1,)), ((), ())), preferred_element_type=F32)
        else:
            x = jnp.dot(perm, page, preferred_element_type=F32)
        for grp in range(n_grp):
            for l in range(CMP_STRIDE):
                hist[grp, l, pl.ds(row, per_page), :] = x[l * per_page:(l + 1) * per_page,
                                                          grp * LANES:(grp + 1) * LANES]

    @pl.when(s == 0)
    def _():
        hist[:, :, n_rows:, :] = jnp.zeros((n_grp, CMP_STRIDE, hist.shape[2] - n_rows, LANES), F32)

    @pl.when(s == n_steps - 1)
    def _():
        for grp in range(n_grp):
            c = grp // 2
            xg = jnp.concatenate([hist[grp, l] for l in range(CMP_STRIDE)], axis=1).astype(BF16)
            fs = jnp.dot(xg, w1_ref[c], preferred_element_type=F32)
            sec[...] = fs[:, 2 * CMP_HIDDEN:]
            hid = _silu(fs[0:n_rows, :2 * CMP_HIDDEN] + sec[1:n_rows + 1, :] + bias_ref[c])
            o_ref[:, grp * LANES:(grp + 1) * LANES] = jnp.dot(hid.astype(BF16), w2_ref[c],
                                                              preferred_element_type=F32)


def _compress(pages, table, w1bd, w2bd, bias2, *, transposed):
    NB, n_pages = table.shape
    n_steps = n_pages // PAGES_PER_STEP
    n_pos = n_pages * PAGE_SIZE
    n_rows = n_pos // CMP_STRIDE

    def page_spec(r):
        return pl.BlockSpec((None,) + pages.shape[1:], lambda b, s, tab: (tab[b, s * PAGES_PER_STEP + r], 0, 0))

    const3 = lambda b, s, tab: (0, 0, 0)
    return pl.pallas_call(
        functools.partial(_compress_kernel, n_pos=n_pos, n_steps=n_steps, transposed=transposed),
        out_shape=jax.ShapeDtypeStruct((NB, n_rows, KV_BRANCH), F32),
        grid_spec=pltpu.PrefetchScalarGridSpec(
            num_scalar_prefetch=1,
            grid=(NB, n_steps),
            in_specs=[page_spec(r) for r in range(PAGES_PER_STEP)] + [
                pl.BlockSpec(w1bd.shape, const3), pl.BlockSpec(bias2.shape, const3), pl.BlockSpec(w2bd.shape, const3)],
            out_specs=pl.BlockSpec((None, n_rows, KV_BRANCH), lambda b, s, tab: (b, 0, 0)),
            scratch_shapes=[pltpu.VMEM((KV_BRANCH // LANES, CMP_STRIDE, n_rows + SUBLANES, LANES), F32),
                            pltpu.VMEM((n_rows + SUBLANES, 2 * CMP_HIDDEN), F32)]),
        compiler_params=_cparams(("parallel", "arbitrary")),
        name="compress",
    )(table, *([pages] * PAGES_PER_STEP), w1bd, bias2, w2bd)


def _overlap(n_cmp_rows, n_sel_cols):
    c0 = jnp.arange(n_cmp_rows) * CMP_STRIDE
    s0 = jnp.arange(n_sel_cols) * SEL_BLOCK
    ov = jnp.minimum(c0[:, None] + CMP_BLOCK, s0[None, :] + SEL_BLOCK) - jnp.maximum(c0[:, None], s0[None, :])
    return (jnp.clip(ov, 0, None).astype(F32) / CMP_BLOCK).astype(BF16)


def _block_expand(n_blocks, n_keys):
    return (jnp.arange(n_keys)[None, :] // SEL_BLOCK == jnp.arange(n_blocks)[:, None]).astype(BF16)


def _masked_softmax(s, mask, axis=-1):
    m = jnp.max(jnp.where(mask, s, NEG), axis=axis, keepdims=True)
    e = jnp.where(mask, jnp.exp(s - m), 0.0)
    d = jnp.sum(e, axis=axis, keepdims=True)
    return e / jnp.where(d > 0, d, 1.0)


def _top_blocks(score, idx, axis):
    n = score.shape[axis]
    sel = jnp.zeros(score.shape, F32)
    work = score
    for _ in range(N_SEL):
        m = jnp.max(work, axis=axis, keepdims=True)
        first = jnp.min(jnp.where(work == m, idx, n), axis=axis, keepdims=True)
        pick = idx == first
        sel = jnp.where(pick & (m > -jnp.inf), 1.0, sel)
        work = jnp.where(pick, -jnp.inf, work)
    return sel


SEL_TK = 512
WIN_KEYS = WINDOW + Q_BLOCK
LOG2E = 1.4426950408889634
V_ROWS = HEAD_DIM_A + 16


def _col_max(s):
    n = s.shape[0]
    parts = jnp.max(s.reshape(SUBLANES, n // SUBLANES, s.shape[1]), axis=0)
    return jnp.max(parts, axis=0, keepdims=True)


def _kv_split_kernel(kv_ref, k_ref, vt_ref):
    k_ref[...] = kv_ref[:, :KV_W].astype(BF16)
    vt = kv_ref[:, KV_W:].T
    tt = vt.shape[1]
    pad_rows = lax.broadcasted_iota(jnp.int32, (V_ROWS - HEAD_DIM_A, tt), 0)
    ones_then_zeros = jnp.where(pad_rows == 0, 1.0, 0.0).astype(BF16)
    for k in range(N_KV_A):
        vt_ref[k * V_ROWS:k * V_ROWS + HEAD_DIM_A, :] = vt[k * HEAD_DIM_A:(k + 1) * HEAD_DIM_A, :].astype(BF16)
        vt_ref[k * V_ROWS + HEAD_DIM_A:(k + 1) * V_ROWS, :] = ones_then_zeros


def _kv_split(z, col, *, tt):
    B, T, _ = z.shape
    return pl.pallas_call(
        _kv_split_kernel,
        out_shape=(jax.ShapeDtypeStruct((B, T, KV_W), BF16), jax.ShapeDtypeStruct((B, N_KV_A * V_ROWS, T), BF16)),
        grid=(B, T // tt),
        in_specs=[pl.BlockSpec((None, tt, KV_BRANCH), lambda b, i: (b, i, col // KV_BRANCH))],
        out_specs=(pl.BlockSpec((None, tt, KV_W), lambda b, i: (b, i, 0)),
                   pl.BlockSpec((None, N_KV_A * V_ROWS, tt), lambda b, i: (b, 0, i))),
        compiler_params=_cparams(("parallel", "parallel")),
        name="kv_split",
    )(z)


def _values_t(v):
    B, n, _ = v.shape
    vt = jnp.swapaxes(v, 1, 2).reshape(B, N_KV_A, HEAD_DIM_A, n)
    pad = jnp.zeros((B, N_KV_A, V_ROWS - HEAD_DIM_A, n), F32).at[:, :, 0].set(1.0)
    return jnp.concatenate([vt, pad], axis=2).reshape(B, N_KV_A * V_ROWS, n).astype(BF16)


def _nsa_prompt_kernel(q_ref, ga_ref, g_ref, ck_ref, cvt_ref, sk_ref, svt_ref, wk_ref, wvt_ref, et_ref, ovt_ref,
                       o_ref, qt_scr, ot_scr, qb_scr, m_scr, acc_scr, s_scr):
    i = pl.program_id(1)
    q0 = i * Q_BLOCK
    n_cmp = ck_ref.shape[0]
    pos_q = q0 + lax.broadcasted_iota(jnp.int32, (1, Q_BLOCK), 1)
    for c in range(D_A // LANES):
        cols = slice(c * LANES, (c + 1) * LANES)
        qt_scr[cols, :] = (q_ref[:, cols].T * (HEAD_DIM_A ** -0.5 * LOG2E)).astype(BF16)
    gates_t = jax.nn.sigmoid(g_ref[...]).T
    j_t = lax.broadcasted_iota(jnp.int32, (LANES, Q_BLOCK), 0)
    cur_t = pos_q // SEL_BLOCK
    valid_t = j_t * SEL_BLOCK <= pos_q
    forced_t = (j_t == 0) | (j_t == cur_t) | (j_t == cur_t - 1)
    cmp_end = lax.broadcasted_iota(jnp.int32, (n_cmp, 1), 0) * CMP_STRIDE + (CMP_BLOCK - 1)
    cbias = jnp.where(cmp_end <= pos_q, 0.0, NEG)
    cvalid = jnp.where(pos_q >= CMP_BLOCK - 1, 1.0, 0.0)
    w0 = pl.multiple_of(jnp.maximum(q0 - WINDOW, 0), Q_BLOCK)
    pos_w = w0 + lax.broadcasted_iota(jnp.int32, (WIN_KEYS, 1), 0)
    wbias = jnp.where((pos_w <= pos_q) & (pos_q - pos_w < WINDOW), 0.0, NEG)
    n_full = q0 // SEL_TK
    zero_half = jnp.zeros((HEAD_DIM_A, Q_BLOCK), BF16)

    def weights_t(s, bias):
        out = []
        for g in range(GROUP_A):
            sg = s[:, g * Q_BLOCK:(g + 1) * Q_BLOCK] + bias
            out.append(jnp.exp2(sg - jnp.max(sg, axis=0, keepdims=True)).astype(BF16))
        return jnp.concatenate(out, axis=1)

    def normalise(o_ext, scale=None):
        d = o_ext[HEAD_DIM_A:HEAD_DIM_A + 1, :]
        inv = 1.0 / jnp.where(d > 0, d, 1.0)
        return inv if scale is None else inv * scale

    def head_scores(k):
        pr = slice((k // 2) * LANES, (k // 2 + 1) * LANES)
        blocks = []
        for g in range(GROUP_A):
            h = k * GROUP_A + g
            qh = qt_scr[h * HEAD_DIM_A:(h + 1) * HEAD_DIM_A, :]
            blocks.append(jnp.concatenate([qh, zero_half] if k % 2 == 0 else [zero_half, qh], axis=0))
        qt = jnp.concatenate(blocks, axis=1)
        return (qt, jnp.dot(ck_ref[:, pr], qt, preferred_element_type=F32),
                jnp.dot(wk_ref[pl.ds(w0, WIN_KEYS), pr], qt, preferred_element_type=F32))

    ahead = head_scores(0)
    for k in range(N_KV_A):
        vr = slice(k * V_ROWS, (k + 1) * V_ROWS)
        qt, s, s_win = ahead
        if k + 1 < N_KV_A:
            ahead = head_scores(k + 1)

        e = weights_t(s, cbias)
        o_ext = jnp.dot(cvt_ref[vr, :], e, preferred_element_type=F32)
        inv = normalise(o_ext, jnp.concatenate([cvalid] * GROUP_A, axis=1))
        o_cmp = o_ext[:HEAD_DIM_A] * inv
        pj = jnp.dot(ovt_ref[...], e, preferred_element_type=F32) * inv
        p_slc = pj[:, 0:Q_BLOCK]
        for g in range(1, GROUP_A):
            p_slc = p_slc + pj[:, g * Q_BLOCK:(g + 1) * Q_BLOCK]

        score_t = jnp.where(valid_t, jnp.where(forced_t, FORCE_SCORE, p_slc), -jnp.inf)
        sel_bias = ((_top_blocks(score_t, j_t, 0) - 1.0) * (-NEG)).astype(BF16)
        qb = jnp.concatenate([qt, jnp.concatenate([sel_bias] * GROUP_A, axis=1)], axis=0)

        qb_scr[k] = qb

        o_ext = jnp.dot(wvt_ref[vr, pl.ds(w0, WIN_KEYS)], weights_t(s_win, wbias), preferred_element_type=F32)
        o_win = o_ext[:HEAD_DIM_A] * normalise(o_ext)

        for g in range(GROUP_A):
            h = k * GROUP_A + g
            lanes = slice(g * Q_BLOCK, (g + 1) * Q_BLOCK)
            ot_scr[h * HEAD_DIM_A:(h + 1) * HEAD_DIM_A, :] = (
                gates_t[h:h + 1, :] * o_cmp[:, lanes]
                + gates_t[2 * N_HEADS_A + h:2 * N_HEADS_A + h + 1, :] * o_win[:, lanes])

    m_scr[...] = jnp.full(m_scr.shape, NEG, F32)
    acc_scr[...] = jnp.zeros(acc_scr.shape, F32)

    def scores(k, t):
        k0 = pl.multiple_of(t * SEL_TK, SEL_TK)
        pr = slice((k // 2) * LANES, (k // 2 + 1) * LANES)
        kb = jnp.concatenate([sk_ref[pl.ds(k0, SEL_TK), pr], et_ref[pl.ds(k0, SEL_TK), :]], axis=1)
        return jnp.dot(kb, qb_scr[k], preferred_element_type=F32)

    for k in range(N_KV_A):
        s_scr[0, k] = scores(k, 0)

    def sel_step(t, causal, slot):
        k0 = pl.multiple_of(t * SEL_TK, SEL_TK)
        if causal:
            pos_k = k0 + lax.broadcasted_iota(jnp.int32, (SEL_TK, 1), 0)
            future = jnp.where(pos_k <= pos_q, 0.0, NEG)
        for k in range(N_KV_A):
            if not causal:
                s_scr[1 - slot, k] = scores(k, t + 1)
            s = s_scr[slot, k]
            es, alphas = [], []
            for g in range(GROUP_A):
                lanes = slice(g * Q_BLOCK, (g + 1) * Q_BLOCK)
                sg = s[:, lanes] + future if causal else s[:, lanes]
                m_old = m_scr[k, :, lanes]
                m_new = jnp.maximum(m_old, _col_max(sg))
                alphas.append(jnp.exp2(m_old - m_new))
                es.append(jnp.exp2(sg - m_new).astype(BF16))
                m_scr[k, :, lanes] = m_new
            pv = jnp.dot(svt_ref[k * V_ROWS:(k + 1) * V_ROWS, pl.ds(k0, SEL_TK)], jnp.concatenate(es, axis=1),
                         preferred_element_type=F32)
            acc_scr[k] = jnp.concatenate(alphas, axis=1) * acc_scr[k] + pv

    def tile_pair(p, carry):
        sel_step(2 * p, False, 0)
        sel_step(2 * p + 1, False, 1)
        return carry

    lax.fori_loop(0, n_full // 2, tile_pair, 0)

    @pl.when(n_full % 2 == 0)
    def _():
        sel_step(n_full, True, 0)

    @pl.when(n_full % 2 == 1)
    def _():
        sel_step(n_full - 1, False, 0)
        sel_step(n_full, True, 1)

    for k in range(N_KV_A):
        acc = acc_scr[k]
        o_sel = acc[:HEAD_DIM_A] * normalise(acc)
        for g in range(GROUP_A):
            h = k * GROUP_A + g
            lanes = slice(g * Q_BLOCK, (g + 1) * Q_BLOCK)
            rows = slice(h * HEAD_DIM_A, (h + 1) * HEAD_DIM_A)
            ot_scr[rows, :] = ot_scr[rows, :] + gates_t[N_HEADS_A + h:N_HEADS_A + h + 1, :] * o_sel[:, lanes]

    for c in range(D_A // LANES):
        cols = slice(c * LANES, (c + 1) * LANES)
        o_ref[:, cols] = (ot_scr[cols, :].T * _silu(ga_ref[:, cols])).astype(o_ref.dtype)


def _nsa_prompt(z, ck, cvt, sk, svt, wk, wvt):
    B, T, _ = z.shape
    n_cmp = ck.shape[1]
    et = _block_expand(LANES, T).T
    ovt = _overlap(n_cmp, LANES).T
    once = pl.Buffered(1)
    gq = GROUP_A * Q_BLOCK
    vt_w = N_KV_A * V_ROWS

    def whole(shape):
        return pl.BlockSpec((None,) + shape, lambda b, i: (b, 0, 0), pipeline_mode=once)

    return pl.pallas_call(
        _nsa_prompt_kernel,
        out_shape=jax.ShapeDtypeStruct((B, T, D_A), BF16),
        grid=(B, T // Q_BLOCK),
        in_specs=[pl.BlockSpec((None, Q_BLOCK, D_A), lambda b, i: (b, i, COL_Q // D_A)),
                  pl.BlockSpec((None, Q_BLOCK, D_A), lambda b, i: (b, i, COL_GA // D_A)),
                  pl.BlockSpec((None, Q_BLOCK, LANES), lambda b, i: (b, i, COL_G // LANES)),
                  whole((n_cmp, KV_W)), whole((vt_w, n_cmp)),
                  whole((T, KV_W)), whole((vt_w, T)), whole((T, KV_W)), whole((vt_w, T)),
                  pl.BlockSpec((T, LANES), lambda b, i: (0, 0), pipeline_mode=once),
                  pl.BlockSpec((LANES, n_cmp), lambda b, i: (0, 0), pipeline_mode=once)],
        out_specs=pl.BlockSpec((None, Q_BLOCK, D_A), lambda b, i: (b, i, 0)),
        scratch_shapes=[pltpu.VMEM((D_A, Q_BLOCK), BF16), pltpu.VMEM((D_A, Q_BLOCK), F32),
                        pltpu.VMEM((N_KV_A, 2 * LANES, gq), BF16),
                        pltpu.VMEM((N_KV_A, 1, gq), F32), pltpu.VMEM((N_KV_A, V_ROWS, gq), F32),
                        pltpu.VMEM((2, N_KV_A, SEL_TK, gq), F32)],
        compiler_params=_cparams(("parallel", "arbitrary")),
        name="nsa_prompt",
    )(z, z, z, ck, cvt, sk, svt, wk, wvt, et, ovt)


def _nsa_sample_kernel(tab_ref, *refs, past, n_steps, t_new):
    pages = refs[:PAGES_PER_STEP]
    (q_ref, ckt_ref, cv_ref, new_ref, wkv_ref, g_ref, ga_ref, e_ref, ov_ref, o_ref, hist) = refs[PAGES_PER_STEP:]
    s = pl.program_id(1)
    for r in range(PAGES_PER_STEP):
        col = pl.multiple_of((s * PAGES_PER_STEP + r) * PAGE_SIZE, PAGE_SIZE)
        hist[:, pl.ds(col, PAGE_SIZE)] = pages[r][...]

    @pl.when(s == 0)
    def _():
        hist[:, past:] = new_ref[...]

    @pl.when(s == n_steps - 1)
    def _():
        n_rows = N_HEADS_A * t_new
        n_keys = hist.shape[1]
        n_cmp = ckt_ref.shape[1]
        n_blk = ov_ref.shape[1]
        n_win = wkv_ref.shape[0]
        q = q_ref[...]
        row = lax.broadcasted_iota(jnp.int32, (n_rows, 1), 0)
        pos_q = past + row % t_new
        row_k = (row // t_new) % N_KV_A

        def diag(o_all):
            out = jnp.zeros((n_rows, HEAD_DIM_A), F32)
            for k in range(N_KV_A):
                out = out + jnp.where(row_k == k, o_all[:, k * HEAD_DIM_A:(k + 1) * HEAD_DIM_A], 0.0)
            return out

        cmp_end = lax.broadcasted_iota(jnp.int32, (1, n_cmp), 1) * CMP_STRIDE + (CMP_BLOCK - 1)
        s_c = jnp.dot(q, ckt_ref[...], preferred_element_type=F32)
        p = _masked_softmax(s_c, cmp_end <= pos_q).astype(BF16)
        o_cmp = diag(jnp.dot(p, cv_ref[...], preferred_element_type=F32))
        pj = jnp.dot(p, ov_ref[...], preferred_element_type=F32)
        rows_kq = N_KV_A * t_new
        p_slc = pj[0:rows_kq]
        for g in range(1, GROUP_A):
            p_slc = p_slc + pj[g * rows_kq:(g + 1) * rows_kq]

        j = lax.broadcasted_iota(jnp.int32, (rows_kq, n_blk), 1)
        pos_r = past + lax.broadcasted_iota(jnp.int32, (rows_kq, n_blk), 0) % t_new
        cur = pos_r // SEL_BLOCK
        valid = j * SEL_BLOCK <= pos_r
        forced = (j == 0) | (j == cur) | (j == cur - 1)
        score = jnp.where(valid, jnp.where(forced, FORCE_SCORE, p_slc), -jnp.inf)
        sel = _top_blocks(score, j, 1)
        sel = jnp.concatenate([sel] * GROUP_A, axis=0).astype(BF16)

        picked = jnp.dot(sel, e_ref[...], preferred_element_type=F32)
        s_s = jnp.dot(q, hist[0:KV_W, :].astype(BF16), preferred_element_type=F32)
        pos_k = lax.broadcasted_iota(jnp.int32, (1, n_keys), 1)
        p = _masked_softmax(s_s, (picked > 0.5) & (pos_k <= pos_q)).astype(BF16)
        o_sel = diag(lax.dot_general(p, hist[KV_W:, :].astype(BF16), (((1,), (1,)), ((), ())),
                                     preferred_element_type=F32))

        wk = wkv_ref[:, 0:KV_W].astype(BF16)
        s_w = lax.dot_general(q, wk, (((1,), (1,)), ((), ())), preferred_element_type=F32)
        pos_w = past - WINDOW + lax.broadcasted_iota(jnp.int32, (1, n_win), 1)
        p = _masked_softmax(s_w, (pos_w <= pos_q) & (pos_q - pos_w < WINDOW)).astype(BF16)
        o_win = diag(jnp.dot(p, wkv_ref[:, KV_W:].astype(BF16), preferred_element_type=F32))

        gates = jax.nn.sigmoid(g_ref[...])
        o = gates[:, 0:1] * o_cmp + gates[:, 1:2] * o_sel + gates[:, 2:3] * o_win
        o_ref[...] = o * _silu(ga_ref[...])


def _nsa_sample(sel_view, table, qbs, ckt, cv, new_sel, wkv, g_rows, ga_rows, *, past, t_new):
    NB, n_pages = table.shape
    n_steps = n_pages // PAGES_PER_STEP
    n_keys = past + PAGE_SIZE
    n_cmp = ckt.shape[2]
    n_blk = 2 * LANES
    n_rows = qbs.shape[1]
    e = _block_expand(n_blk, n_keys)
    ov = _overlap(n_cmp, n_blk)

    def page_spec(r):
        return pl.BlockSpec((None, KV_BRANCH, PAGE_SIZE), lambda b, s, tab: (tab[b, s * PAGES_PER_STEP + r], 0, 0))

    def per_b(arr):
        return pl.BlockSpec((None,) + arr.shape[1:], lambda b, s, tab: (b, 0, 0))

    const2 = lambda b, s, tab: (0, 0)
    return pl.pallas_call(
        functools.partial(_nsa_sample_kernel, past=past, n_steps=n_steps, t_new=t_new),
        out_shape=jax.ShapeDtypeStruct((NB, n_rows, HEAD_DIM_A), F32),
        grid_spec=pltpu.PrefetchScalarGridSpec(
            num_scalar_prefetch=1,
            grid=(NB, n_steps),
            in_specs=[page_spec(r) for r in range(PAGES_PER_STEP)] + [
                per_b(qbs), per_b(ckt), per_b(cv), per_b(new_sel), per_b(wkv), per_b(g_rows), per_b(ga_rows),
                pl.BlockSpec(e.shape, const2, pipeline_mode=pl.Buffered(1)),
                pl.BlockSpec(ov.shape, const2, pipeline_mode=pl.Buffered(1))],
            out_specs=pl.BlockSpec((None, n_rows, HEAD_DIM_A), lambda b, s, tab: (b, 0, 0)),
            scratch_shapes=[pltpu.VMEM((KV_BRANCH, n_keys), F32)]),
        compiler_params=_cparams(("parallel", "arbitrary")),
        name="nsa_sample",
    )(table, *([sel_view] * PAGES_PER_STEP), qbs, ckt, cv, new_sel, wkv, g_rows, ga_rows, e, ov)


def _permute_w_in_a(w):
    c_kv, c_g, c_ga, c_glu, c_gb = 1024, 2560, 2608, 3632, 5680
    cols = [w[:, :c_kv], w[:, c_ga:c_glu], w[:, c_glu:c_gb], w[:, c_gb:], w[:, c_kv:c_g], w[:, c_g:c_ga]]
    n = sum(c.shape[1] for c in cols)
    return jnp.concatenate(cols + [jnp.zeros((w.shape[0], EVEN_N - n), w.dtype)], axis=1).astype(BF16)


def _pad_rows(x, front, total):
    return jnp.pad(x, ((0, 0), (front, total - front - x.shape[1]), (0, 0)))


def _even_prompt(z, cmpw, conv_args, *, tq):
    B, T, _ = z.shape
    kv = z[:, :, COL_KV:COL_G]
    kv_cmp, kv_sel, kv_win = (kv[:, :, br * KV_BRANCH:(br + 1) * KV_BRANCH] for br in range(3))
    n_pages = T // PAGE_SIZE
    table = jnp.arange(B * n_pages, dtype=jnp.int32).reshape(B, n_pages)
    ckv = _compress(kv_cmp.reshape(B * n_pages, PAGE_SIZE, KV_BRANCH), table, *cmpw, transposed=False)

    def split_kv(x):
        return x[:, :, :KV_W].astype(BF16), _values_t(x[:, :, KV_W:])

    o_a = _nsa_prompt(z, *split_kv(ckv), *_kv_split(z, COL_KV + KV_BRANCH, tt=tq),
                      *_kv_split(z, COL_KV + 2 * KV_BRANCH, tt=tq))
    u = _glu(z, tt=tq)
    halo_map = lambda bb, i: (bb, jnp.maximum(i * (tq // CONV_HALO) - 1, 0), 0)
    o_b = _conv(u, halo_map, True, u, z, COL_GB // D_MODEL, *conv_args, tt=tq)
    kv_shape = (B, T, 2, N_KV_A, HEAD_DIM_A)
    w_keep = min(WINDOW, T)
    outs = (kv_cmp.reshape(kv_shape), kv_sel.reshape(kv_shape),
            kv_win[:, T - w_keep:].reshape(B, w_keep, 2, N_KV_A, HEAD_DIM_A), u[:, T - (CONV_WIDTH - 1):])
    return o_a, o_b, outs


def _even_sample(z, cmpw, conv_args, cmp_view, sel_view, table, win_buf, conv_hist, *, past):
    NB, t_new, _ = z.shape
    kv = z[:, :, COL_KV:COL_G]
    kv_cmp, kv_sel, kv_win = (kv[:, :, br * KV_BRANCH:(br + 1) * KV_BRANCH] for br in range(3))
    ckv = _compress(cmp_view, table, *cmpw, transposed=True)
    q = z[:, :, COL_Q:COL_Q + D_A].reshape(NB, t_new, N_KV_A, GROUP_A, HEAD_DIM_A) * (HEAD_DIM_A ** -0.5)
    qbs = jnp.einsum('bqkgd,kK->bgkqKd', q, jnp.eye(N_KV_A, dtype=F32)).reshape(NB, N_HEADS_A * t_new, KV_W)
    g_rows = z[:, :, COL_G:COL_G + 3 * N_HEADS_A].reshape(NB, t_new, 3, N_KV_A, GROUP_A)
    g_rows = jnp.transpose(g_rows, (0, 4, 3, 1, 2)).reshape(NB, N_HEADS_A * t_new, 3)
    g_rows = jnp.pad(g_rows, ((0, 0), (0, 0), (0, LANES - 3)))
    ga_rows = z[:, :, COL_GA:COL_GA + D_A].reshape(NB, t_new, N_KV_A, GROUP_A, HEAD_DIM_A)
    ga_rows = jnp.transpose(ga_rows, (0, 3, 2, 1, 4)).reshape(NB, N_HEADS_A * t_new, HEAD_DIM_A)
    wkv = jnp.concatenate([win_buf.reshape(NB, -1, KV_BRANCH), kv_win], axis=1)
    wb = win_buf.shape[1]
    o = _nsa_sample(sel_view, table, qbs.astype(BF16),
                    jnp.swapaxes(ckv[:, :, :KV_W], 1, 2).astype(BF16), ckv[:, :, KV_W:].astype(BF16),
                    jnp.swapaxes(_pad_rows(kv_sel, 0, PAGE_SIZE), 1, 2), _pad_rows(wkv, 0, WIN_KEYS), g_rows, ga_rows,
                    past=past, t_new=t_new)
    o_a = jnp.transpose(o.reshape(NB, GROUP_A, N_KV_A, t_new, HEAD_DIM_A), (0, 3, 2, 1, 4))
    o_a = o_a.reshape(NB * t_new, D_A).astype(BF16)
    u = _glu(z.reshape(1, NB * t_new, EVEN_N), tt=NB * t_new).reshape(NB, t_new, D_MODEL)
    halo = _pad_rows(conv_hist, CONV_HALO - (CONV_WIDTH - 1), CONV_HALO)
    gate_b = _pad_rows(z[:, :, COL_GB:COL_GB + D_MODEL], 0, SUBLANES)
    o_b = _conv(halo, lambda bb, i: (bb, 0, 0), False, _pad_rows(u, 0, SUBLANES), gate_b, 0, *conv_args, tt=SUBLANES)
    o_b = o_b[:, :t_new].reshape(NB * t_new, D_MODEL)
    kv_shape = (NB, t_new, 2, N_KV_A, HEAD_DIM_A)
    outs = (kv_cmp.reshape(kv_shape), kv_sel.reshape(kv_shape),
            wkv[:, t_new:].reshape(NB, wb, 2, N_KV_A, HEAD_DIM_A),
            jnp.concatenate([conv_hist, u], axis=1)[:, -(CONV_WIDTH - 1):])
    return o_a, o_b, outs


def kernel(x_prompt, x_sample, mem_prompt, cache_cmp_kv, cache_sel_kv, cache_win_kv, state_conv, state_pool,
           cache_mem_kv, page_table, norm_mix, norm_mem, norm_out, w_in_a, w_out_a, cmp_pe, cmp_w1, cmp_w2,
           conv_w, conv_b, conv_ln_g, conv_ln_b, w_in_c, pool_w, pool_scale, w_out_c, w_mem_q, w_mem_kv, w_mem_o):
    B, T, D = x_prompt.shape
    NB, t_new, _ = x_sample.shape
    depth = norm_mix.shape[0]
    n_pool = cache_cmp_kv.shape[1]
    past = page_table.shape[1] * PAGE_SIZE
    mem_len = mem_prompt.shape[1]
    tm_p = 1024
    tq = 256
    ms = NB * t_new
    xp = x_prompt.reshape(B * T, D)
    xs = x_sample.reshape(ms, D)
    p_cmp, p_sel, p_win, p_conv, p_pool, p_mem = [], [], [], [], [], []
    s_cmp, s_sel, s_win, s_conv, s_pool = [], [], [], [], []

    def pages_t(cache):
        return jnp.transpose(cache, (0, 1, 3, 4, 5, 2)).reshape(-1, KV_BRANCH, PAGE_SIZE)

    cmp_view = pages_t(cache_cmp_kv)
    sel_view = pages_t(cache_sel_kv)

    for layer in range(depth):
        if layer % 2 == 0:
            a = layer // 2
            w_in = _permute_w_in_a(w_in_a[a])
            w_out = w_out_a[a].astype(BF16)
            cmpw = _compress_weights(cmp_pe[a], cmp_w1[a], cmp_w2[a])
            conv_args = (conv_w[a], conv_b[a], conv_ln_g[a], conv_ln_b[a])
            zp = _mm([xp], w_in, gain=norm_mix[layer], tm=tm_p, tn=EVEN_TN).reshape(B, T, EVEN_N)
            o_a, o_b, outs = _even_prompt(zp, cmpw, conv_args, tq=tq)
            xp = _mm([o_a.reshape(B * T, D_A), o_b.reshape(B * T, D)], w_out, res=xp, tm=tm_p, tn=D)
            for lst, val in zip((p_cmp, p_sel, p_win, p_conv), outs):
                lst.append(val)
            zs = _mm([xs], w_in, gain=norm_mix[layer], tm=ms, tn=EVEN_TN).reshape(NB, t_new, EVEN_N)
            table = page_table + a * n_pool
            o_a, o_b, outs = _even_sample(zs, cmpw, conv_args, cmp_view, sel_view, table, cache_win_kv[a],
                                          state_conv[a], past=past)
            xs = _mm([o_a, o_b], w_out, res=xs, tm=ms, tn=D)
            for lst, val in zip((s_cmp, s_sel, s_win, s_conv), outs):
                lst.append(val)
        else:
            c = layer // 2
            w_in = w_in_c[c].astype(BF16)
            w_out = w_out_c[c].astype(BF16)
            zp = _mm([xp], w_in, gain=norm_mix[layer], tm=tm_p, tn=D).reshape(B, T, 2 * D)
            halo_map = lambda bb, i: (bb, jnp.maximum(i * (tq // POOL_HALO) - 1, 0), 0)
            m = _pool(zp, halo_map, True, zp, pool_w[c], pool_scale[c], tt=tq, pos0=0)
            xp = _mm([m.reshape(B * T, D)], w_out, res=xp, tm=tm_p, tn=D)
            p_pool.append(zp[:, T - POOL_HIST:, :D])
            zs = _mm([xs], w_in, gain=norm_mix[layer], tm=ms, tn=D).reshape(NB, t_new, 2 * D)
            hist = state_pool[c]
            m = _pool(_pad_rows(hist, POOL_HALO - POOL_HIST, POOL_HALO), lambda bb, i: (bb, 0, 0), False,
                      _pad_rows(zs, 0, SUBLANES), pool_w[c], pool_scale[c], tt=SUBLANES, pos0=past)
            xs = _mm([m[:, :t_new].reshape(ms, D)], w_out, res=xs, tm=ms, tn=D)
            s_pool.append(jnp.concatenate([hist, zs[:, :, :D]], axis=1)[:, -POOL_HIST:])

        wq = w_mem_q[layer].astype(BF16)
        wo = w_mem_o[layer].astype(BF16)
        mkv = _mm([mem_prompt.reshape(B * mem_len, D)], w_mem_kv[layer].astype(BF16), tm=B * mem_len, tn=D)
        mkv = mkv.reshape(B, mem_len, 2 * D)
        p_mem.append(mkv.reshape(B, mem_len, 2, N_HEADS_M, HEAD_DIM_M))
        q = _mm([xp], wq, gain=norm_mem[layer], tm=tm_p, tn=D, out_dtype=BF16).reshape(B, T, D)
        o = _mem_attn(q, jnp.swapaxes(mkv[:, :, :D], 1, 2).astype(BF16), mkv[:, :, D:].astype(BF16), tm=tq)
        xp = _mm([o.reshape(B * T, D)], wo, res=xp, tm=tm_p, tn=D)
        ckv = cache_mem_kv[layer].reshape(NB, mem_len, 2 * D)
        q = _mm([xs], wq, gain=norm_mem[layer], tm=ms, tn=D, out_dtype=BF16).reshape(NB, t_new, D)
        o = _mem_attn(_pad_rows(q, 0, SUBLANES), jnp.swapaxes(ckv[:, :, :D], 1, 2).astype(BF16),
                      ckv[:, :, D:].astype(BF16), tm=SUBLANES)
        xs = _mm([o[:, :t_new].reshape(ms, D)], wo, res=xs, tm=ms, tn=D)

    y_prompt = _rmsnorm(xp, norm_out, tm=tm_p).reshape(B, T, D)
    y_sample = _rmsnorm(xs, norm_out, tm=ms).reshape(NB, t_new, D)
    return (y_prompt, y_sample,
            jnp.stack(p_cmp), jnp.stack(p_sel), jnp.stack(p_win), jnp.stack(p_conv), jnp.stack(p_pool),
            jnp.stack(p_mem),
            jnp.stack(s_cmp), jnp.stack(s_sel), jnp.stack(s_win), jnp.stack(s_conv), jnp.stack(s_pool))
```

```python
import functools

import jax
import jax.numpy as jnp
from jax import lax
from jax.experimental import pallas as pl
from jax.experimental.pallas import tpu as pltpu

F32 = jnp.float32
BF16 = jnp.bfloat16

LANES = 128
SUBLANES = 8
VMEM_LIMIT_BYTES = 56 * 1024 * 1024

D_MODEL = 1024
N_HEADS_A = 16
HEAD_DIM_A = 64
N_KV_A = 4
GROUP_A = N_HEADS_A // N_KV_A
D_A = N_HEADS_A * HEAD_DIM_A
KV_W = N_KV_A * HEAD_DIM_A
KV_BRANCH = 2 * KV_W
CMP_STRIDE = 16
CMP_BLOCK = 32
CMP_HIDDEN = 128
SEL_BLOCK = 64
N_SEL = 16
WINDOW = 512
Q_BLOCK = 128
FORCE_SCORE = 1000.0
CONV_WIDTH = 31
CONV_HALO = 32
POOL_WINDOWS = (2, 4, 8, 16)
POOL_GROUP = D_MODEL // len(POOL_WINDOWS)
POOL_HIST = 15
POOL_HALO = 16
PAGE_SIZE = 128
N_HEADS_M = 4
HEAD_DIM_M = 256
NORM_EPS = 1e-6
NEG = -1e30

COL_Q, COL_GA, COL_VAL, COL_GL, COL_GB = 0, 1024, 2048, 3072, 4096
COL_KV = 5120
COL_G = COL_KV + 3 * KV_BRANCH
EVEN_N = 6912
EVEN_TN = 768
PAGES_PER_STEP = 32


def _cparams(sem):
    return pltpu.CompilerParams(dimension_semantics=sem, vmem_limit_bytes=VMEM_LIMIT_BYTES)


def _silu(x):
    return x * jax.nn.sigmoid(x)


def _mm_kernel(*refs, n_parts, has_gain, has_res):
    parts = refs[:n_parts]
    pos = n_parts
    gain_ref = refs[pos] if has_gain else None
    pos += int(has_gain)
    w_ref = refs[pos]
    pos += 1
    res_ref = refs[pos] if has_res else None
    pos += int(has_res)
    o_ref, lhs = refs[pos], refs[pos + 1]

    @pl.when(pl.program_id(1) == 0)
    def _():
        off = 0
        for p in parts:
            x = p[...].astype(F32)
            if has_gain:
                x = x * lax.rsqrt(jnp.mean(x * x, axis=-1, keepdims=True) + NORM_EPS) * gain_ref[...]
            lhs[:, off:off + p.shape[1]] = x.astype(BF16)
            off += p.shape[1]

    acc = jnp.dot(lhs[...], w_ref[...], preferred_element_type=F32)
    if has_res:
        acc = acc + res_ref[...]
    o_ref[...] = acc.astype(o_ref.dtype)


def _mm(parts, w, *, gain=None, res=None, tm, tn, out_dtype=F32):
    M = parts[0].shape[0]
    K, N = w.shape
    assert sum(p.shape[1] for p in parts) == K and M % tm == 0 and N % tn == 0
    in_specs = [pl.BlockSpec((tm, p.shape[1]), lambda i, j: (i, 0)) for p in parts]
    args = list(parts)
    if gain is not None:
        in_specs.append(pl.BlockSpec((1, K), lambda i, j: (0, 0)))
        args.append(gain.reshape(1, K).astype(F32))
    in_specs.append(pl.BlockSpec((K, tn), lambda i, j: (0, j)))
    args.append(w)
    if res is not None:
        in_specs.append(pl.BlockSpec((tm, tn), lambda i, j: (i, j)))
        args.append(res)
    return pl.pallas_call(
        functools.partial(_mm_kernel, n_parts=len(parts), has_gain=gain is not None, has_res=res is not None),
        out_shape=jax.ShapeDtypeStruct((M, N), out_dtype),
        grid=(M // tm, N // tn),
        in_specs=in_specs,
        out_specs=pl.BlockSpec((tm, tn), lambda i, j: (i, j)),
        scratch_shapes=[pltpu.VMEM((tm, K), BF16)],
        compiler_params=_cparams(("parallel", "arbitrary")),
        name="mm",
    )(*args)


def _rmsnorm_kernel(x_ref, g_ref, o_ref):
    x = x_ref[...]
    o_ref[...] = x * lax.rsqrt(jnp.mean(x * x, axis=-1, keepdims=True) + NORM_EPS) * g_ref[...]


def _rmsnorm(x, g, *, tm):
    M, D = x.shape
    return pl.pallas_call(
        _rmsnorm_kernel,
        out_shape=jax.ShapeDtypeStruct((M, D), F32),
        grid=(M // tm,),
        in_specs=[pl.BlockSpec((tm, D), lambda i: (i, 0)), pl.BlockSpec((1, D), lambda i: (0, 0))],
        out_specs=pl.BlockSpec((tm, D), lambda i: (i, 0)),
        compiler_params=_cparams(("parallel",)),
        name="rmsnorm",
    )(x, g.reshape(1, D))


def _glu_kernel(val_ref, gl_ref, o_ref):
    o_ref[...] = val_ref[...] * jax.nn.sigmoid(gl_ref[...])


def _glu(z, *, tt):
    B, T, _ = z.shape
    blk = (None, tt, D_MODEL)
    return pl.pallas_call(
        _glu_kernel,
        out_shape=jax.ShapeDtypeStruct((B, T, D_MODEL), F32),
        grid=(B, T // tt),
        in_specs=[pl.BlockSpec(blk, lambda b, i: (b, i, COL_VAL // D_MODEL)),
                  pl.BlockSpec(blk, lambda b, i: (b, i, COL_GL // D_MODEL))],
        out_specs=pl.BlockSpec(blk, lambda b, i: (b, i, 0)),
        compiler_params=_cparams(("parallel", "parallel")),
        name="glu",
    )(z, z)


CONV_ROWS = 32


def _conv_kernel(halo_ref, cur_ref, gate_ref, w_ref, b_ref, lg_ref, lb_ref, o_ref, buf, sh, y_scr, *, tt, zero_first_halo):
    halo = halo_ref[...]
    if zero_first_halo:
        halo = jnp.where(pl.program_id(1) == 0, 0.0, halo)
    buf[0:CONV_HALO, :] = halo
    buf[CONV_HALO:CONV_HALO + tt, :] = cur_ref[...]
    buf[CONV_HALO + tt:, :] = jnp.zeros((SUBLANES, D_MODEL), F32)
    first = CONV_HALO - (CONV_WIDTH - 1)
    n_sh = sh.shape[1]
    for r in range(SUBLANES):
        sh[r] = buf[first + r:first + r + n_sh, :]
    rc = min(CONV_ROWS, tt)
    for r0 in range(0, tt, rc):
        acc = jnp.broadcast_to(b_ref[...], (rc, D_MODEL))
        for j in range(CONV_WIDTH):
            a0 = r0 + (j // SUBLANES) * SUBLANES
            acc = acc + w_ref[j:j + 1, :] * sh[j % SUBLANES, a0:a0 + rc, :]
        y_scr[r0:r0 + rc, :] = acc
    y = y_scr[...]
    xc = y - jnp.mean(y, axis=-1, keepdims=True)
    yn = xc * lax.rsqrt(jnp.mean(xc * xc, axis=-1, keepdims=True) + NORM_EPS) * lg_ref[...] + lb_ref[...]
    o_ref[...] = (_silu(yn) * _silu(gate_ref[...])).astype(o_ref.dtype)


def _conv(halo_arr, halo_map, zero_first_halo, u, gate_arr, gate_col, w, b, lg, lb, *, tt):
    B, T, _ = u.shape
    w32 = jnp.concatenate([w, jnp.zeros((CONV_HALO - CONV_WIDTH + 1, D_MODEL), F32)], axis=0)[:CONV_HALO]
    vec = pl.BlockSpec((1, D_MODEL), lambda bb, i: (0, 0))
    return pl.pallas_call(
        functools.partial(_conv_kernel, tt=tt, zero_first_halo=zero_first_halo),
        out_shape=jax.ShapeDtypeStruct((B, T, D_MODEL), BF16),
        grid=(B, T // tt),
        in_specs=[pl.BlockSpec((None, CONV_HALO, D_MODEL), halo_map),
                  pl.BlockSpec((None, tt, D_MODEL), lambda bb, i: (bb, i, 0)),
                  pl.BlockSpec((None, tt, D_MODEL), lambda bb, i: (bb, i, gate_col)),
                  pl.BlockSpec((CONV_HALO, D_MODEL), lambda bb, i: (0, 0)),
                  vec, vec, vec],
        out_specs=pl.BlockSpec((None, tt, D_MODEL), lambda bb, i: (bb, i, 0)),
        scratch_shapes=[pltpu.VMEM((CONV_HALO + tt + SUBLANES, D_MODEL), F32),
                        pltpu.VMEM((SUBLANES, tt + CONV_HALO - SUBLANES, D_MODEL), F32),
                        pltpu.VMEM((tt, D_MODEL), F32)],
        compiler_params=_cparams(("parallel", "parallel")),
        name="conv",
    )(halo_arr, u, gate_arr, w32, b.reshape(1, -1), lg.reshape(1, -1), lb.reshape(1, -1))


def _pool_kernel(halo_ref, cur_ref, gate_ref, pw_ref, ps_ref, o_ref, buf, *, tt, zero_first_halo, pos0):
    i = pl.program_id(1)
    halo = halo_ref[...]
    if zero_first_halo:
        halo = jnp.where(i == 0, 0.0, halo)
    buf[0:POOL_HALO, :] = halo
    buf[POOL_HALO:POOL_HALO + tt, :] = cur_ref[...]
    pos = pos0 + i * tt + lax.broadcasted_iota(jnp.int32, (tt, 1), 0)
    for g, w in enumerate(POOL_WINDOWS):
        c0, c1 = g * POOL_GROUP, (g + 1) * POOL_GROUP
        s = buf[POOL_HALO:POOL_HALO + tt, c0:c1]
        for j in range(1, w):
            s = s + buf[POOL_HALO - j:POOL_HALO - j + tt, c0:c1]
        cnt = jnp.minimum(w, pos + 1).astype(F32)
        d = s / cnt - buf[POOL_HALO:POOL_HALO + tt, c0:c1]
        m = jnp.dot(d.astype(BF16), pw_ref[g], preferred_element_type=F32)
        o_ref[:, c0:c1] = (m * ps_ref[:, c0:c1] * _silu(gate_ref[:, c0:c1])).astype(o_ref.dtype)


def _pool(halo_arr, halo_map, zero_first_halo, z, pool_w, pool_scale, *, tt, pos0):
    B, T, _ = z.shape
    return pl.pallas_call(
        functools.partial(_pool_kernel, tt=tt, zero_first_halo=zero_first_halo, pos0=pos0),
        out_shape=jax.ShapeDtypeStruct((B, T, D_MODEL), BF16),
        grid=(B, T // tt),
        in_specs=[pl.BlockSpec((None, POOL_HALO, D_MODEL), halo_map),
                  pl.BlockSpec((None, tt, D_MODEL), lambda bb, i: (bb, i, 0)),
                  pl.BlockSpec((None, tt, D_MODEL), lambda bb, i: (bb, i, 1)),
                  pl.BlockSpec((len(POOL_WINDOWS), POOL_GROUP, POOL_GROUP), lambda bb, i: (0, 0, 0)),
                  pl.BlockSpec((1, D_MODEL), lambda bb, i: (0, 0))],
        out_specs=pl.BlockSpec((None, tt, D_MODEL), lambda bb, i: (bb, i, 0)),
        scratch_shapes=[pltpu.VMEM((POOL_HALO + tt, D_MODEL), F32)],
        compiler_params=_cparams(("parallel", "parallel")),
        name="pool",
    )(halo_arr, z, z, pool_w.astype(BF16), pool_scale.reshape(1, -1))


def _mem_attn_kernel(q_ref, kt_ref, v_ref, o_ref):
    for h in range(N_HEADS_M):
        c0, c1 = h * HEAD_DIM_M, (h + 1) * HEAD_DIM_M
        q = q_ref[:, c0:c1] * (HEAD_DIM_M ** -0.5)
        s = jnp.dot(q, kt_ref[c0:c1, :], preferred_element_type=F32)
        e = jnp.exp(s - jnp.max(s, axis=-1, keepdims=True))
        p = e / jnp.sum(e, axis=-1, keepdims=True)
        o = jnp.dot(p.astype(BF16), v_ref[:, c0:c1], preferred_element_type=F32)
        o_ref[:, c0:c1] = o.astype(o_ref.dtype)


def _mem_attn(q, kt, v, *, tm):
    NB, Tq, D = q.shape
    M = v.shape[1]
    return pl.pallas_call(
        _mem_attn_kernel,
        out_shape=jax.ShapeDtypeStruct((NB, Tq, D), BF16),
        grid=(NB, Tq // tm),
        in_specs=[pl.BlockSpec((None, tm, D), lambda b, i: (b, i, 0)),
                  pl.BlockSpec((None, D, M), lambda b, i: (b, 0, 0)),
                  pl.BlockSpec((None, M, D), lambda b, i: (b, 0, 0))],
        out_specs=pl.BlockSpec((None, tm, D), lambda b, i: (b, i, 0)),
        compiler_params=_cparams(("parallel", "parallel")),
        name="mem_attn",
    )(q, kt, v)


def _compress_weights(pe, w1, w2):
    eye2 = jnp.eye(2, dtype=F32)

    def bd(wc):
        return jnp.einsum('ldh,kK->lkdKh', wc, eye2).reshape(CMP_STRIDE * LANES, 2 * CMP_HIDDEN)

    w1bd = jnp.stack([jnp.concatenate([bd(w1[c, :CMP_STRIDE]), bd(w1[c, CMP_STRIDE:])], axis=1)
                      for c in range(2)]).astype(BF16)
    w2bd = jnp.stack([jnp.einsum('hd,kK->khKd', w2[c], eye2).reshape(2 * CMP_HIDDEN, LANES)
                      for c in range(2)]).astype(BF16)
    pe_rows = jnp.zeros((2, 2 * SUBLANES, CMP_BLOCK * HEAD_DIM_A), F32).at[:, 0].set(pe.reshape(2, -1))
    bias = jnp.stack([_mm([pe_rows[c]], w1[c].reshape(CMP_BLOCK * HEAD_DIM_A, CMP_HIDDEN).astype(BF16),
                          tm=2 * SUBLANES, tn=CMP_HIDDEN)[0] for c in range(2)])
    bias2 = jnp.concatenate([bias, bias], axis=1).reshape(2, 1, 2 * CMP_HIDDEN)
    return w1bd, w2bd, bias2


def _compress_kernel(tab_ref, *refs, n_pos, n_steps, transposed):
    pages = refs[:PAGES_PER_STEP]
    w1_ref, bias_ref, w2_ref, o_ref, hist, sec = refs[PAGES_PER_STEP:]
    s = pl.program_id(1)
    n_grp = KV_BRANCH // LANES
    per_page = PAGE_SIZE // CMP_STRIDE
    n_rows = n_pos // CMP_STRIDE
    out_row = lax.broadcasted_iota(jnp.int32, (PAGE_SIZE, PAGE_SIZE), 0)
    in_pos = lax.broadcasted_iota(jnp.int32, (PAGE_SIZE, PAGE_SIZE), 1)
    perm = (in_pos == (out_row % per_page) * CMP_STRIDE + out_row // per_page).astype(BF16)
    for r in range(PAGES_PER_STEP):
        row = pl.multiple_of((s * PAGES_PER_STEP + r) * per_page, per_page)
        page = pages[r][...].astype(BF16)
        if transposed:
            x = lax.dot_general(perm, page, (((1,), (1,)), ((), ())), preferred_element_type=F32)
        else:
            x = jnp.dot(perm, page, preferred_element_type=F32)
        for grp in range(n_grp):
            for l in range(CMP_STRIDE):
                hist[grp, l, pl.ds(row, per_page), :] = x[l * per_page:(l + 1) * per_page,
                                                          grp * LANES:(grp + 1) * LANES]

    @pl.when(s == 0)
    def _():
        hist[:, :, n_rows:, :] = jnp.zeros((n_grp, CMP_STRIDE, hist.shape[2] - n_rows, LANES), F32)

    @pl.when(s == n_steps - 1)
    def _():
        for grp in range(n_grp):
            c = grp // 2
            xg = jnp.concatenate([hist[grp, l] for l in range(CMP_STRIDE)], axis=1).astype(BF16)
            fs = jnp.dot(xg, w1_ref[c], preferred_element_type=F32)
            sec[...] = fs[:, 2 * CMP_HIDDEN:]
            hid = _silu(fs[0:n_rows, :2 * CMP_HIDDEN] + sec[1:n_rows + 1, :] + bias_ref[c])
            o_ref[:, grp * LANES:(grp + 1) * LANES] = jnp.dot(hid.astype(BF16), w2_ref[c],
                                                              preferred_element_type=F32)


def _compress(pages, table, w1bd, w2bd, bias2, *, transposed):
    NB, n_pages = table.shape
    n_steps = n_pages // PAGES_PER_STEP
    n_pos = n_pages * PAGE_SIZE
    n_rows = n_pos // CMP_STRIDE

    def page_spec(r):
        return pl.BlockSpec((None,) + pages.shape[1:], lambda b, s, tab: (tab[b, s * PAGES_PER_STEP + r], 0, 0))

    const3 = lambda b, s, tab: (0, 0, 0)
    return pl.pallas_call(
        functools.partial(_compress_kernel, n_pos=n_pos, n_steps=n_steps, transposed=transposed),
        out_shape=jax.ShapeDtypeStruct((NB, n_rows, KV_BRANCH), F32),
        grid_spec=pltpu.PrefetchScalarGridSpec(
            num_scalar_prefetch=1,
            grid=(NB, n_steps),
            in_specs=[page_spec(r) for r in range(PAGES_PER_STEP)] + [
                pl.BlockSpec(w1bd.shape, const3), pl.BlockSpec(bias2.shape, const3), pl.BlockSpec(w2bd.shape, const3)],
            out_specs=pl.BlockSpec((None, n_rows, KV_BRANCH), lambda b, s, tab: (b, 0, 0)),
            scratch_shapes=[pltpu.VMEM((KV_BRANCH // LANES, CMP_STRIDE, n_rows + SUBLANES, LANES), F32),
                            pltpu.VMEM((n_rows + SUBLANES, 2 * CMP_HIDDEN), F32)]),
        compiler_params=_cparams(("parallel", "arbitrary")),
        name="compress",
    )(table, *([pages] * PAGES_PER_STEP), w1bd, bias2, w2bd)


def _overlap(n_cmp_rows, n_sel_cols):
    c0 = jnp.arange(n_cmp_rows) * CMP_STRIDE
    s0 = jnp.arange(n_sel_cols) * SEL_BLOCK
    ov = jnp.minimum(c0[:, None] + CMP_BLOCK, s0[None, :] + SEL_BLOCK) - jnp.maximum(c0[:, None], s0[None, :])
    return (jnp.clip(ov, 0, None).astype(F32) / CMP_BLOCK).astype(BF16)


def _block_expand(n_blocks, n_keys):
    return (jnp.arange(n_keys)[None, :] // SEL_BLOCK == jnp.arange(n_blocks)[:, None]).astype(BF16)


def _masked_softmax(s, mask, axis=-1):
    m = jnp.max(jnp.where(mask, s, NEG), axis=axis, keepdims=True)
    e = jnp.where(mask, jnp.exp(s - m), 0.0)
    d = jnp.sum(e, axis=axis, keepdims=True)
    return e / jnp.where(d > 0, d, 1.0)


def _top_blocks(score, idx, axis):
    n = score.shape[axis]
    work = score
    for _ in range(N_SEL):
        m = jnp.max(work, axis=axis, keepdims=True)
        first = jnp.min(jnp.where(work == m, idx, n), axis=axis, keepdims=True)
        work = jnp.where(idx == first, -jnp.inf, work)
    return jnp.where((work == -jnp.inf) & (score > -jnp.inf), 1.0, 0.0)


SEL_TK = 512
WIN_KEYS = WINDOW + Q_BLOCK
LOG2E = 1.4426950408889634
V_ROWS = HEAD_DIM_A + 16


def _col_max(s):
    n = s.shape[0]
    parts = jnp.max(s.reshape(SUBLANES, n // SUBLANES, s.shape[1]), axis=0)
    return jnp.max(parts, axis=0, keepdims=True)


def _kv_split_kernel(kv_ref, k_ref, vt_ref):
    k_ref[...] = kv_ref[:, :KV_W].astype(BF16)
    vt = kv_ref[:, KV_W:].T
    tt = vt.shape[1]
    pad_rows = lax.broadcasted_iota(jnp.int32, (V_ROWS - HEAD_DIM_A, tt), 0)
    ones_then_zeros = jnp.where(pad_rows == 0, 1.0, 0.0).astype(BF16)
    for k in range(N_KV_A):
        vt_ref[k * V_ROWS:k * V_ROWS + HEAD_DIM_A, :] = vt[k * HEAD_DIM_A:(k + 1) * HEAD_DIM_A, :].astype(BF16)
        vt_ref[k * V_ROWS + HEAD_DIM_A:(k + 1) * V_ROWS, :] = ones_then_zeros


def _kv_split(z, col, *, tt):
    B, T, _ = z.shape
    return pl.pallas_call(
        _kv_split_kernel,
        out_shape=(jax.ShapeDtypeStruct((B, T, KV_W), BF16), jax.ShapeDtypeStruct((B, N_KV_A * V_ROWS, T), BF16)),
        grid=(B, T // tt),
        in_specs=[pl.BlockSpec((None, tt, KV_BRANCH), lambda b, i: (b, i, col // KV_BRANCH))],
        out_specs=(pl.BlockSpec((None, tt, KV_W), lambda b, i: (b, i, 0)),
                   pl.BlockSpec((None, N_KV_A * V_ROWS, tt), lambda b, i: (b, 0, i))),
        compiler_params=_cparams(("parallel", "parallel")),
        name="kv_split",
    )(z)


def _values_t(v):
    B, n, _ = v.shape
    vt = jnp.swapaxes(v, 1, 2).reshape(B, N_KV_A, HEAD_DIM_A, n)
    pad = jnp.zeros((B, N_KV_A, V_ROWS - HEAD_DIM_A, n), F32).at[:, :, 0].set(1.0)
    return jnp.concatenate([vt, pad], axis=2).reshape(B, N_KV_A * V_ROWS, n).astype(BF16)


def _nsa_prompt_kernel(q_ref, ga_ref, g_ref, ck_ref, cvt_ref, sk_ref, svt_ref, wk_ref, wvt_ref, et_ref, ovt_ref,
                       o_ref, qt_scr, ot_scr, qb_scr, m_scr, acc_scr, s_scr):
    i = pl.program_id(1)
    q0 = i * Q_BLOCK
    n_cmp = ck_ref.shape[0]
    pos_q = q0 + lax.broadcasted_iota(jnp.int32, (1, Q_BLOCK), 1)
    for c in range(D_A // LANES):
        cols = slice(c * LANES, (c + 1) * LANES)
        qt_scr[cols, :] = (q_ref[:, cols].T * (HEAD_DIM_A ** -0.5 * LOG2E)).astype(BF16)
    gates_t = jax.nn.sigmoid(g_ref[...]).T
    j_t = lax.broadcasted_iota(jnp.int32, (LANES, Q_BLOCK), 0)
    cur_t = pos_q // SEL_BLOCK
    valid_t = j_t * SEL_BLOCK <= pos_q
    forced_t = (j_t == 0) | (j_t == cur_t) | (j_t == cur_t - 1)
    cmp_end = lax.broadcasted_iota(jnp.int32, (n_cmp, 1), 0) * CMP_STRIDE + (CMP_BLOCK - 1)
    cbias = jnp.where(cmp_end <= pos_q, 0.0, NEG)
    cvalid = jnp.where(pos_q >= CMP_BLOCK - 1, 1.0, 0.0)
    w0 = pl.multiple_of(jnp.maximum(q0 - WINDOW, 0), Q_BLOCK)
    pos_w = w0 + lax.broadcasted_iota(jnp.int32, (WIN_KEYS, 1), 0)
    wbias = jnp.where((pos_w <= pos_q) & (pos_q - pos_w < WINDOW), 0.0, NEG)
    n_full = q0 // SEL_TK
    zero_half = jnp.zeros((HEAD_DIM_A, Q_BLOCK), BF16)

    def weights_t(s, bias):
        out = []
        for g in range(GROUP_A):
            sg = s[:, g * Q_BLOCK:(g + 1) * Q_BLOCK] + bias
            out.append(jnp.exp2(sg - jnp.max(sg, axis=0, keepdims=True)).astype(BF16))
        return jnp.concatenate(out, axis=1)

    def normalise(o_ext, scale=None):
        d = o_ext[HEAD_DIM_A:HEAD_DIM_A + 1, :]
        inv = 1.0 / jnp.where(d > 0, d, 1.0)
        return inv if scale is None else inv * scale

    def head_scores(k):
        pr = slice((k // 2) * LANES, (k // 2 + 1) * LANES)
        blocks = []
        for g in range(GROUP_A):
            h = k * GROUP_A + g
            qh = qt_scr[h * HEAD_DIM_A:(h + 1) * HEAD_DIM_A, :]
            blocks.append(jnp.concatenate([qh, zero_half] if k % 2 == 0 else [zero_half, qh], axis=0))
        qt = jnp.concatenate(blocks, axis=1)
        return (qt, jnp.dot(ck_ref[:, pr], qt, preferred_element_type=F32),
                jnp.dot(wk_ref[pl.ds(w0, WIN_KEYS), pr], qt, preferred_element_type=F32))

    ahead = head_scores(0)
    for k in range(N_KV_A):
        vr = slice(k * V_ROWS, (k + 1) * V_ROWS)
        qt, s, s_win = ahead
        if k + 1 < N_KV_A:
            ahead = head_scores(k + 1)

        e = weights_t(s, cbias)
        o_ext = jnp.dot(cvt_ref[vr, :], e, preferred_element_type=F32)
        inv = normalise(o_ext, jnp.concatenate([cvalid] * GROUP_A, axis=1))
        o_cmp = o_ext[:HEAD_DIM_A] * inv
        pj = jnp.dot(ovt_ref[...], e, preferred_element_type=F32) * inv
        p_slc = pj[:, 0:Q_BLOCK]
        for g in range(1, GROUP_A):
            p_slc = p_slc + pj[:, g * Q_BLOCK:(g + 1) * Q_BLOCK]

        score_t = jnp.where(valid_t, jnp.where(forced_t, FORCE_SCORE, p_slc), -jnp.inf)
        sel_bias = ((_top_blocks(score_t, j_t, 0) - 1.0) * (-NEG)).astype(BF16)
        qb = jnp.concatenate([qt, jnp.concatenate([sel_bias] * GROUP_A, axis=1)], axis=0)

        qb_scr[k] = qb

        o_ext = jnp.dot(wvt_ref[vr, pl.ds(w0, WIN_KEYS)], weights_t(s_win, wbias), preferred_element_type=F32)
        o_win = o_ext[:HEAD_DIM_A] * normalise(o_ext)

        for g in range(GROUP_A):
            h = k * GROUP_A + g
            lanes = slice(g * Q_BLOCK, (g + 1) * Q_BLOCK)
            ot_scr[h * HEAD_DIM_A:(h + 1) * HEAD_DIM_A, :] = (
                gates_t[h:h + 1, :] * o_cmp[:, lanes]
                + gates_t[2 * N_HEADS_A + h:2 * N_HEADS_A + h + 1, :] * o_win[:, lanes])

    m_scr[...] = jnp.full(m_scr.shape, NEG, F32)
    acc_scr[...] = jnp.zeros(acc_scr.shape, F32)

    def scores(k, t):
        k0 = pl.multiple_of(t * SEL_TK, SEL_TK)
        pr = slice((k // 2) * LANES, (k // 2 + 1) * LANES)
        kb = jnp.concatenate([sk_ref[pl.ds(k0, SEL_TK), pr], et_ref[pl.ds(k0, SEL_TK), :]], axis=1)
        return jnp.dot(kb, qb_scr[k], preferred_element_type=F32)

    for k in range(N_KV_A):
        s_scr[0, k] = scores(k, 0)

    def sel_step(t, causal, slot):
        k0 = pl.multiple_of(t * SEL_TK, SEL_TK)
        if causal:
            pos_k = k0 + lax.broadcasted_iota(jnp.int32, (SEL_TK, 1), 0)
            future = jnp.where(pos_k <= pos_q, 0.0, NEG)
        for k in range(N_KV_A):
            if not causal:
                s_scr[1 - slot, k] = scores(k, t + 1)
            s = s_scr[slot, k]
            es, alphas = [], []
            for g in range(GROUP_A):
                lanes = slice(g * Q_BLOCK, (g + 1) * Q_BLOCK)
                sg = s[:, lanes] + future if causal else s[:, lanes]
                m_old = m_scr[k, :, lanes]
                m_new = jnp.maximum(m_old, _col_max(sg))
                alphas.append(jnp.exp2(m_old - m_new))
                es.append(jnp.exp2(sg - m_new).astype(BF16))
                m_scr[k, :, lanes] = m_new
            pv = jnp.dot(svt_ref[k * V_ROWS:(k + 1) * V_ROWS, pl.ds(k0, SEL_TK)], jnp.concatenate(es, axis=1),
                         preferred_element_type=F32)
            acc_scr[k] = jnp.concatenate(alphas, axis=1) * acc_scr[k] + pv

    def tile_pair(p, carry):
        sel_step(2 * p, False, 0)
        sel_step(2 * p + 1, False, 1)
        return carry

    lax.fori_loop(0, n_full // 2, tile_pair, 0)

    @pl.when(n_full % 2 == 0)
    def _():
        sel_step(n_full, True, 0)

    @pl.when(n_full % 2 == 1)
    def _():
        sel_step(n_full - 1, False, 0)
        sel_step(n_full, True, 1)

    for k in range(N_KV_A):
        acc = acc_scr[k]
        o_sel = acc[:HEAD_DIM_A] * normalise(acc)
        for g in range(GROUP_A):
            h = k * GROUP_A + g
            lanes = slice(g * Q_BLOCK, (g + 1) * Q_BLOCK)
            rows = slice(h * HEAD_DIM_A, (h + 1) * HEAD_DIM_A)
            ot_scr[rows, :] = ot_scr[rows, :] + gates_t[N_HEADS_A + h:N_HEADS_A + h + 1, :] * o_sel[:, lanes]

    for c in range(D_A // LANES):
        cols = slice(c * LANES, (c + 1) * LANES)
        o_ref[:, cols] = (ot_scr[cols, :].T * _silu(ga_ref[:, cols])).astype(o_ref.dtype)


def _nsa_prompt(z, ck, cvt, sk, svt, wk, wvt):
    B, T, _ = z.shape
    n_cmp = ck.shape[1]
    et = _block_expand(LANES, T).T
    ovt = _overlap(n_cmp, LANES).T
    once = pl.Buffered(1)
    gq = GROUP_A * Q_BLOCK
    vt_w = N_KV_A * V_ROWS

    def whole(shape):
        return pl.BlockSpec((None,) + shape, lambda b, i: (b, 0, 0), pipeline_mode=once)

    return pl.pallas_call(
        _nsa_prompt_kernel,
        out_shape=jax.ShapeDtypeStruct((B, T, D_A), BF16),
        grid=(B, T // Q_BLOCK),
        in_specs=[pl.BlockSpec((None, Q_BLOCK, D_A), lambda b, i: (b, i, COL_Q // D_A)),
                  pl.BlockSpec((None, Q_BLOCK, D_A), lambda b, i: (b, i, COL_GA // D_A)),
                  pl.BlockSpec((None, Q_BLOCK, LANES), lambda b, i: (b, i, COL_G // LANES)),
                  whole((n_cmp, KV_W)), whole((vt_w, n_cmp)),
                  whole((T, KV_W)), whole((vt_w, T)), whole((T, KV_W)), whole((vt_w, T)),
                  pl.BlockSpec((T, LANES), lambda b, i: (0, 0), pipeline_mode=once),
                  pl.BlockSpec((LANES, n_cmp), lambda b, i: (0, 0), pipeline_mode=once)],
        out_specs=pl.BlockSpec((None, Q_BLOCK, D_A), lambda b, i: (b, i, 0)),
        scratch_shapes=[pltpu.VMEM((D_A, Q_BLOCK), BF16), pltpu.VMEM((D_A, Q_BLOCK), F32),
                        pltpu.VMEM((N_KV_A, 2 * LANES, gq), BF16),
                        pltpu.VMEM((N_KV_A, 1, gq), F32), pltpu.VMEM((N_KV_A, V_ROWS, gq), F32),
                        pltpu.VMEM((2, N_KV_A, SEL_TK, gq), F32)],
        compiler_params=_cparams(("parallel", "arbitrary")),
        name="nsa_prompt",
    )(z, z, z, ck, cvt, sk, svt, wk, wvt, et, ovt)


def _nsa_sample_kernel(tab_ref, *refs, past, n_steps, t_new):
    pages = refs[:PAGES_PER_STEP]
    (q_ref, ckt_ref, cv_ref, new_ref, wkv_ref, g_ref, ga_ref, e_ref, ov_ref, o_ref, hist) = refs[PAGES_PER_STEP:]
    s = pl.program_id(1)
    for r in range(PAGES_PER_STEP):
        col = pl.multiple_of((s * PAGES_PER_STEP + r) * PAGE_SIZE, PAGE_SIZE)
        hist[:, pl.ds(col, PAGE_SIZE)] = pages[r][...]

    @pl.when(s == 0)
    def _():
        hist[:, past:] = new_ref[...]

    @pl.when(s == n_steps - 1)
    def _():
        n_rows = N_HEADS_A * t_new
        n_keys = hist.shape[1]
        n_cmp = ckt_ref.shape[1]
        n_blk = ov_ref.shape[1]
        n_win = wkv_ref.shape[0]
        q = q_ref[...]
        row = lax.broadcasted_iota(jnp.int32, (n_rows, 1), 0)
        pos_q = past + row % t_new
        row_k = (row // t_new) % N_KV_A

        def diag(o_all):
            out = jnp.zeros((n_rows, HEAD_DIM_A), F32)
            for k in range(N_KV_A):
                out = out + jnp.where(row_k == k, o_all[:, k * HEAD_DIM_A:(k + 1) * HEAD_DIM_A], 0.0)
            return out

        cmp_end = lax.broadcasted_iota(jnp.int32, (1, n_cmp), 1) * CMP_STRIDE + (CMP_BLOCK - 1)
        s_c = jnp.dot(q, ckt_ref[...], preferred_element_type=F32)
        p = _masked_softmax(s_c, cmp_end <= pos_q).astype(BF16)
        o_cmp = diag(jnp.dot(p, cv_ref[...], preferred_element_type=F32))
        pj = jnp.dot(p, ov_ref[...], preferred_element_type=F32)
        rows_kq = N_KV_A * t_new
        p_slc = pj[0:rows_kq]
        for g in range(1, GROUP_A):
            p_slc = p_slc + pj[g * rows_kq:(g + 1) * rows_kq]

        j = lax.broadcasted_iota(jnp.int32, (rows_kq, n_blk), 1)
        pos_r = past + lax.broadcasted_iota(jnp.int32, (rows_kq, n_blk), 0) % t_new
        cur = pos_r // SEL_BLOCK
        valid = j * SEL_BLOCK <= pos_r
        forced = (j == 0) | (j == cur) | (j == cur - 1)
        score = jnp.where(valid, jnp.where(forced, FORCE_SCORE, p_slc), -jnp.inf)
        sel = _top_blocks(score, j, 1)
        sel = jnp.concatenate([sel] * GROUP_A, axis=0).astype(BF16)

        picked = jnp.dot(sel, e_ref[...], preferred_element_type=F32)
        s_s = jnp.dot(q, hist[0:KV_W, :].astype(BF16), preferred_element_type=F32)
        pos_k = lax.broadcasted_iota(jnp.int32, (1, n_keys), 1)
        p = _masked_softmax(s_s, (picked > 0.5) & (pos_k <= pos_q)).astype(BF16)
        o_sel = diag(lax.dot_general(p, hist[KV_W:, :].astype(BF16), (((1,), (1,)), ((), ())),
                                     preferred_element_type=F32))

        wk = wkv_ref[:, 0:KV_W].astype(BF16)
        s_w = lax.dot_general(q, wk, (((1,), (1,)), ((), ())), preferred_element_type=F32)
        pos_w = past - WINDOW + lax.broadcasted_iota(jnp.int32, (1, n_win), 1)
        p = _masked_softmax(s_w, (pos_w <= pos_q) & (pos_q - pos_w < WINDOW)).astype(BF16)
        o_win = diag(jnp.dot(p, wkv_ref[:, KV_W:].astype(BF16), preferred_element_type=F32))

        gates = jax.nn.sigmoid(g_ref[...])
        o = gates[:, 0:1] * o_cmp + gates[:, 1:2] * o_sel + gates[:, 2:3] * o_win
        o_ref[...] = o * _silu(ga_ref[...])


def _nsa_sample(sel_view, table, qbs, ckt, cv, new_sel, wkv, g_rows, ga_rows, *, past, t_new):
    NB, n_pages = table.shape
    n_steps = n_pages // PAGES_PER_STEP
    n_keys = past + PAGE_SIZE
    n_cmp = ckt.shape[2]
    n_blk = 2 * LANES
    n_rows = qbs.shape[1]
    e = _block_expand(n_blk, n_keys)
    ov = _overlap(n_cmp, n_blk)

    def page_spec(r):
        return pl.BlockSpec((None, KV_BRANCH, PAGE_SIZE), lambda b, s, tab: (tab[b, s * PAGES_PER_STEP + r], 0, 0))

    def per_b(arr):
        return pl.BlockSpec((None,) + arr.shape[1:], lambda b, s, tab: (b, 0, 0))

    const2 = lambda b, s, tab: (0, 0)
    return pl.pallas_call(
        functools.partial(_nsa_sample_kernel, past=past, n_steps=n_steps, t_new=t_new),
        out_shape=jax.ShapeDtypeStruct((NB, n_rows, HEAD_DIM_A), F32),
        grid_spec=pltpu.PrefetchScalarGridSpec(
            num_scalar_prefetch=1,
            grid=(NB, n_steps),
            in_specs=[page_spec(r) for r in range(PAGES_PER_STEP)] + [
                per_b(qbs), per_b(ckt), per_b(cv), per_b(new_sel), per_b(wkv), per_b(g_rows), per_b(ga_rows),
                pl.BlockSpec(e.shape, const2, pipeline_mode=pl.Buffered(1)),
                pl.BlockSpec(ov.shape, const2, pipeline_mode=pl.Buffered(1))],
            out_specs=pl.BlockSpec((None, n_rows, HEAD_DIM_A), lambda b, s, tab: (b, 0, 0)),
            scratch_shapes=[pltpu.VMEM((KV_BRANCH, n_keys), F32)]),
        compiler_params=_cparams(("parallel", "arbitrary")),
        name="nsa_sample",
    )(table, *([sel_view] * PAGES_PER_STEP), qbs, ckt, cv, new_sel, wkv, g_rows, ga_rows, e, ov)


def _permute_w_in_a(w):
    c_kv, c_g, c_ga, c_glu, c_gb = 1024, 2560, 2608, 3632, 5680
    cols = [w[:, :c_kv], w[:, c_ga:c_glu], w[:, c_glu:c_gb], w[:, c_gb:], w[:, c_kv:c_g], w[:, c_g:c_ga]]
    n = sum(c.shape[1] for c in cols)
    return jnp.concatenate(cols + [jnp.zeros((w.shape[0], EVEN_N - n), w.dtype)], axis=1).astype(BF16)


def _pad_rows(x, front, total):
    return jnp.pad(x, ((0, 0), (front, total - front - x.shape[1]), (0, 0)))


def _even_prompt(z, cmpw, conv_args, *, tq):
    B, T, _ = z.shape
    kv = z[:, :, COL_KV:COL_G]
    kv_cmp, kv_sel, kv_win = (kv[:, :, br * KV_BRANCH:(br + 1) * KV_BRANCH] for br in range(3))
    n_pages = T // PAGE_SIZE
    table = jnp.arange(B * n_pages, dtype=jnp.int32).reshape(B, n_pages)
    ckv = _compress(kv_cmp.reshape(B * n_pages, PAGE_SIZE, KV_BRANCH), table, *cmpw, transposed=False)

    def split_kv(x):
        return x[:, :, :KV_W].astype(BF16), _values_t(x[:, :, KV_W:])

    o_a = _nsa_prompt(z, *split_kv(ckv), *_kv_split(z, COL_KV + KV_BRANCH, tt=tq),
                      *_kv_split(z, COL_KV + 2 * KV_BRANCH, tt=tq))
    u = _glu(z, tt=tq)
    halo_map = lambda bb, i: (bb, jnp.maximum(i * (tq // CONV_HALO) - 1, 0), 0)
    o_b = _conv(u, halo_map, True, u, z, COL_GB // D_MODEL, *conv_args, tt=tq)
    kv_shape = (B, T, 2, N_KV_A, HEAD_DIM_A)
    w_keep = min(WINDOW, T)
    outs = (kv_cmp.reshape(kv_shape), kv_sel.reshape(kv_shape),
            kv_win[:, T - w_keep:].reshape(B, w_keep, 2, N_KV_A, HEAD_DIM_A), u[:, T - (CONV_WIDTH - 1):])
    return o_a, o_b, outs


def _even_sample(z, cmpw, conv_args, cmp_view, sel_view, table, win_buf, conv_hist, *, past):
    NB, t_new, _ = z.shape
    kv = z[:, :, COL_KV:COL_G]
    kv_cmp, kv_sel, kv_win = (kv[:, :, br * KV_BRANCH:(br + 1) * KV_BRANCH] for br in range(3))
    ckv = _compress(cmp_view, table, *cmpw, transposed=True)
    q = z[:, :, COL_Q:COL_Q + D_A].reshape(NB, t_new, N_KV_A, GROUP_A, HEAD_DIM_A) * (HEAD_DIM_A ** -0.5)
    qbs = jnp.einsum('bqkgd,kK->bgkqKd', q, jnp.eye(N_KV_A, dtype=F32)).reshape(NB, N_HEADS_A * t_new, KV_W)
    g_rows = z[:, :, COL_G:COL_G + 3 * N_HEADS_A].reshape(NB, t_new, 3, N_KV_A, GROUP_A)
    g_rows = jnp.transpose(g_rows, (0, 4, 3, 1, 2)).reshape(NB, N_HEADS_A * t_new, 3)
    g_rows = jnp.pad(g_rows, ((0, 0), (0, 0), (0, LANES - 3)))
    ga_rows = z[:, :, COL_GA:COL_GA + D_A].reshape(NB, t_new, N_KV_A, GROUP_A, HEAD_DIM_A)
    ga_rows = jnp.transpose(ga_rows, (0, 3, 2, 1, 4)).reshape(NB, N_HEADS_A * t_new, HEAD_DIM_A)
    wkv = jnp.concatenate([win_buf.reshape(NB, -1, KV_BRANCH), kv_win], axis=1)
    wb = win_buf.shape[1]
    o = _nsa_sample(sel_view, table, qbs.astype(BF16),
                    jnp.swapaxes(ckv[:, :, :KV_W], 1, 2).astype(BF16), ckv[:, :, KV_W:].astype(BF16),
                    jnp.swapaxes(_pad_rows(kv_sel, 0, PAGE_SIZE), 1, 2), _pad_rows(wkv, 0, WIN_KEYS), g_rows, ga_rows,
                    past=past, t_new=t_new)
    o_a = jnp.transpose(o.reshape(NB, GROUP_A, N_KV_A, t_new, HEAD_DIM_A), (0, 3, 2, 1, 4))
    o_a = o_a.reshape(NB * t_new, D_A).astype(BF16)
    u = _glu(z.reshape(1, NB * t_new, EVEN_N), tt=NB * t_new).reshape(NB, t_new, D_MODEL)
    halo = _pad_rows(conv_hist, CONV_HALO - (CONV_WIDTH - 1), CONV_HALO)
    gate_b = _pad_rows(z[:, :, COL_GB:COL_GB + D_MODEL], 0, SUBLANES)
    o_b = _conv(halo, lambda bb, i: (bb, 0, 0), False, _pad_rows(u, 0, SUBLANES), gate_b, 0, *conv_args, tt=SUBLANES)
    o_b = o_b[:, :t_new].reshape(NB * t_new, D_MODEL)
    kv_shape = (NB, t_new, 2, N_KV_A, HEAD_DIM_A)
    outs = (kv_cmp.reshape(kv_shape), kv_sel.reshape(kv_shape),
            wkv[:, t_new:].reshape(NB, wb, 2, N_KV_A, HEAD_DIM_A),
            jnp.concatenate([conv_hist, u], axis=1)[:, -(CONV_WIDTH - 1):])
    return o_a, o_b, outs


def kernel(x_prompt, x_sample, mem_prompt, cache_cmp_kv, cache_sel_kv, cache_win_kv, state_conv, state_pool,
           cache_mem_kv, page_table, norm_mix, norm_mem, norm_out, w_in_a, w_out_a, cmp_pe, cmp_w1, cmp_w2,
           conv_w, conv_b, conv_ln_g, conv_ln_b, w_in_c, pool_w, pool_scale, w_out_c, w_mem_q, w_mem_kv, w_mem_o):
    B, T, D = x_prompt.shape
    NB, t_new, _ = x_sample.shape
    depth = norm_mix.shape[0]
    n_pool = cache_cmp_kv.shape[1]
    past = page_table.shape[1] * PAGE_SIZE
    mem_len = mem_prompt.shape[1]
    tm_p = 1024
    tq = 512
    ms = NB * t_new
    xp = x_prompt.reshape(B * T, D)
    xs = x_sample.reshape(ms, D)
    p_cmp, p_sel, p_win, p_conv, p_pool, p_mem = [], [], [], [], [], []
    s_cmp, s_sel, s_win, s_conv, s_pool = [], [], [], [], []

    def pages_t(cache):
        return jnp.transpose(cache, (0, 1, 3, 4, 5, 2)).reshape(-1, KV_BRANCH, PAGE_SIZE)

    cmp_view = pages_t(cache_cmp_kv)
    sel_view = pages_t(cache_sel_kv)

    for layer in range(depth):
        if layer % 2 == 0:
            a = layer // 2
            w_in = _permute_w_in_a(w_in_a[a])
            w_out = w_out_a[a].astype(BF16)
            cmpw = _compress_weights(cmp_pe[a], cmp_w1[a], cmp_w2[a])
            conv_args = (conv_w[a], conv_b[a], conv_ln_g[a], conv_ln_b[a])
            zp = _mm([xp], w_in, gain=norm_mix[layer], tm=tm_p, tn=EVEN_TN).reshape(B, T, EVEN_N)
            o_a, o_b, outs = _even_prompt(zp, cmpw, conv_args, tq=tq)
            xp = _mm([o_a.reshape(B * T, D_A), o_b.reshape(B * T, D)], w_out, res=xp, tm=tm_p, tn=D)
            for lst, val in zip((p_cmp, p_sel, p_win, p_conv), outs):
                lst.append(val)
            zs = _mm([xs], w_in, gain=norm_mix[layer], tm=ms, tn=EVEN_TN).reshape(NB, t_new, EVEN_N)
            table = page_table + a * n_pool
            o_a, o_b, outs = _even_sample(zs, cmpw, conv_args, cmp_view, sel_view, table, cache_win_kv[a],
                                          state_conv[a], past=past)
            xs = _mm([o_a, o_b], w_out, res=xs, tm=ms, tn=D)
            for lst, val in zip((s_cmp, s_sel, s_win, s_conv), outs):
                lst.append(val)
        else:
            c = layer // 2
            w_in = w_in_c[c].astype(BF16)
            w_out = w_out_c[c].astype(BF16)
            zp = _mm([xp], w_in, gain=norm_mix[layer], tm=tm_p, tn=D).reshape(B, T, 2 * D)
            halo_map = lambda bb, i: (bb, jnp.maximum(i * (tq // POOL_HALO) - 1, 0), 0)
            m = _pool(zp, halo_map, True, zp, pool_w[c], pool_scale[c], tt=tq, pos0=0)
            xp = _mm([m.reshape(B * T, D)], w_out, res=xp, tm=tm_p, tn=D)
            p_pool.append(zp[:, T - POOL_HIST:, :D])
            zs = _mm([xs], w_in, gain=norm_mix[layer], tm=ms, tn=D).reshape(NB, t_new, 2 * D)
            hist = state_pool[c]
            m = _pool(_pad_rows(hist, POOL_HALO - POOL_HIST, POOL_HALO), lambda bb, i: (bb, 0, 0), False,
                      _pad_rows(zs, 0, SUBLANES), pool_w[c], pool_scale[c], tt=SUBLANES, pos0=past)
            xs = _mm([m[:, :t_new].reshape(ms, D)], w_out, res=xs, tm=ms, tn=D)
            s_pool.append(jnp.concatenate([hist, zs[:, :, :D]], axis=1)[:, -POOL_HIST:])

        wq = w_mem_q[layer].astype(BF16)
        wo = w_mem_o[layer].astype(BF16)
        mkv = _mm([mem_prompt.reshape(B * mem_len, D)], w_mem_kv[layer].astype(BF16), tm=B * mem_len, tn=D)
        mkv = mkv.reshape(B, mem_len, 2 * D)
        p_mem.append(mkv.reshape(B, mem_len, 2, N_HEADS_M, HEAD_DIM_M))
        q = _mm([xp], wq, gain=norm_mem[layer], tm=tm_p, tn=D, out_dtype=BF16).reshape(B, T, D)
        o = _mem_attn(q, jnp.swapaxes(mkv[:, :, :D], 1, 2).astype(BF16), mkv[:, :, D:].astype(BF16), tm=tq)
        xp = _mm([o.reshape(B * T, D)], wo, res=xp, tm=tm_p, tn=D)
        ckv = cache_mem_kv[layer].reshape(NB, mem_len, 2 * D)
        q = _mm([xs], wq, gain=norm_mem[layer], tm=ms, tn=D, out_dtype=BF16).reshape(NB, t_new, D)
        o = _mem_attn(_pad_rows(q, 0, SUBLANES), jnp.swapaxes(ckv[:, :, :D], 1, 2).astype(BF16),
                      ckv[:, :, D:].astype(BF16), tm=SUBLANES)
        xs = _mm([o[:, :t_new].reshape(ms, D)], wo, res=xs, tm=ms, tn=D)

    y_prompt = _rmsnorm(xp, norm_out, tm=tm_p).reshape(B, T, D)
    y_sample = _rmsnorm(xs, norm_out, tm=ms).reshape(NB, t_new, D)
    return (y_prompt, y_sample,
            jnp.stack(p_cmp), jnp.stack(p_sel), jnp.stack(p_win), jnp.stack(p_conv), jnp.stack(p_pool),
            jnp.stack(p_mem),
            jnp.stack(s_cmp), jnp.stack(s_sel), jnp.stack(s_win), jnp.stack(s_conv), jnp.stack(s_pool))
```
